```python
import math
import jax, jax.numpy as jnp
from jax import lax
import numpy as np

D_MODEL = 2048
BATCH = 8
SEQ = 4096
DEPTH = 4

CTX_LEN = 256
GRID_W = 64
MIX_WIDTH = D_MODEL
POOL_WIDTH = 3 * MIX_WIDTH // 4
SSM_WIDTH = MIX_WIDTH - POOL_WIDTH
POOL_WINDOWS = (2, 4, 8, 16)
N_POOL_GROUPS = len(POOL_WINDOWS)
POOL_GROUP = POOL_WIDTH // N_POOL_GROUPS
SSM_GROUP = 16
N_SSM_GROUPS = SSM_WIDTH // SSM_GROUP
SSM_STATE = 64
D_FF = 5632
CONV_K = 3
DT_MIN, DT_MAX = 1e-3, 1e-1
EPS = 1e-6

kernel_name = "hybrid_pool_s5_prefix_dit_block"


def rms_norm(x, gain):
    xf = x.astype(jnp.float32)
    y = xf * lax.rsqrt(jnp.mean(xf * xf, axis=-1, keepdims=True) + EPS)
    return (y * gain.astype(jnp.float32)).astype(x.dtype)


def modulate(h, shift, scale):
    return h * (1 + scale[:, None, :]) + shift[:, None, :]


def multiscale_pool(u, w_pool, pool_scale):
    bsz, n, _ = u.shape
    uf = u.astype(jnp.float32)
    cs = jnp.concatenate([jnp.zeros((bsz, 1, POOL_WIDTH), jnp.float32), jnp.cumsum(uf, axis=1)], axis=1)
    t = jnp.arange(n)
    parts = []
    for g, w in enumerate(POOL_WINDOWS):
        lo = jnp.maximum(t - w // 2, 0)
        hi = jnp.minimum(t + w // 2, n)
        sl = slice(g * POOL_GROUP, (g + 1) * POOL_GROUP)
        csg = cs[..., sl]
        cnt = (hi - lo).astype(jnp.float32)[None, :, None]
        mean = (jnp.take(csg, hi, axis=1) - jnp.take(csg, lo, axis=1)) / cnt
        parts.append(mean - uf[..., sl])
    p = jnp.stack(parts, axis=2)
    y = jnp.einsum('blgc,gcd->blgd', p, w_pool.astype(jnp.float32))
    return (y.reshape(bsz, n, POOL_WIDTH) * pool_scale.astype(jnp.float32)).astype(u.dtype)


def s5_discretise(a_re, a_im, log_dt, b_re, b_im):
    a_re = a_re.astype(jnp.float32)
    a_im = a_im.astype(jnp.float32)
    dt = jnp.exp(log_dt.astype(jnp.float32))[:, None]
    mag = jnp.exp(a_re * dt)
    lam_re = mag * jnp.cos(a_im * dt)
    lam_im = mag * jnp.sin(a_im * dt)
    denom = a_re * a_re + a_im * a_im
    nr, ni = lam_re - 1.0, lam_im
    f_re = (nr * a_re + ni * a_im) / denom
    f_im = (ni * a_re - nr * a_im) / denom
    b_re = b_re.astype(jnp.float32)
    b_im = b_im.astype(jnp.float32)
    bb_re = f_re[..., None] * b_re - f_im[..., None] * b_im
    bb_im = f_re[..., None] * b_im + f_im[..., None] * b_re
    return lam_re, lam_im, bb_re, bb_im


def _complex_linear_recurrence_op(left, right):
    a1r, a1i, b1r, b1i = left
    a2r, a2i, b2r, b2i = right
    return (a2r * a1r - a2i * a1i,
            a2r * a1i + a2i * a1r,
            a2r * b1r - a2i * b1i + b2r,
            a2r * b1i + a2i * b1r + b2i)


def s5_scan(u_g, lam_re, lam_im, bb_re, bb_im, h0, reverse):
    b_re = jnp.einsum('blgh,gph->blgp', u_g, bb_re)
    b_im = jnp.einsum('blgh,gph->blgp', u_g, bb_im)
    if h0 is not None:
        pos = -1 if reverse else 0
        h0_re, h0_im = h0
        b_re = b_re.at[:, pos].add(lam_re * h0_re - lam_im * h0_im)
        b_im = b_im.at[:, pos].add(lam_re * h0_im + lam_im * h0_re)
    a_re = jnp.broadcast_to(lam_re, b_re.shape)
    a_im = jnp.broadcast_to(lam_im, b_im.shape)
    _, _, h_re, h_im = lax.associative_scan(
        _complex_linear_recurrence_op, (a_re, a_im, b_re, b_im), axis=1, reverse=reverse)
    return h_re, h_im


def s5_readout(h, c_re, c_im):
    h_re, h_im = h
    return (jnp.einsum('blgp,ghp->blgh', h_re, c_re.astype(jnp.float32))
            - jnp.einsum('blgp,ghp->blgh', h_im, c_im.astype(jnp.float32)))


def to_ssm_groups(u_ssm):
    bsz, n, _ = u_ssm.shape
    return u_ssm.astype(jnp.float32).reshape(bsz, n, N_SSM_GROUPS, SSM_GROUP)


def s5_head_output(u_ssm, y, ssm_d, w_glu):
    bsz, n, _ = u_ssm.shape
    yf = y.reshape(bsz, n, SSM_WIDTH) + ssm_d.astype(jnp.float32) * u_ssm.astype(jnp.float32)
    yf = jax.nn.gelu(yf)
    return (yf * jax.nn.sigmoid(yf @ w_glu.astype(jnp.float32))).astype(u_ssm.dtype)


def mix_project(u, ssm_y, w_pool, pool_scale, ssm_d, w_glu, w_out):
    pool_out = multiscale_pool(u[..., :POOL_WIDTH], w_pool, pool_scale)
    ssm_out = s5_head_output(u[..., POOL_WIDTH:], ssm_y, ssm_d, w_glu)
    return jnp.concatenate([pool_out, ssm_out], axis=-1) @ w_out


def conv_glu_ffn(h, w_up, w_conv, w_down, rows):
    bsz, n, _ = h.shape
    z = h @ w_up
    if rows is None:
        grid = z[:, None]
        k = w_conv[1:2]
    else:
        grid = z.reshape(bsz, rows, GRID_W, 2 * D_FF)
        k = w_conv
    grid = lax.conv_general_dilated(grid, k[:, :, None, :], (1, 1), 'SAME',
                                    dimension_numbers=('NHWC', 'HWIO', 'NHWC'),
                                    feature_group_count=2 * D_FF)
    val, gate = jnp.split(grid.reshape(bsz, n, 2 * D_FF), 2, axis=-1)
    return (val * jax.nn.silu(gate)) @ w_down


def setup_inputs(seed: int = 0) -> dict:
    key = jax.random.key(seed)
    ks = jax.random.split(key, 26)
    f32 = jnp.float32
    nrm = lambda k, shape, s: jax.random.normal(k, shape, f32) * s
    G, P, H = N_SSM_GROUPS, SSM_STATE, SSM_GROUP
    a_im_base = jnp.pi * jnp.arange(P, dtype=f32)
    return {
        "x": nrm(ks[0], (BATCH, SEQ, D_MODEL), 1.0),
        "c": nrm(ks[1], (BATCH, D_MODEL), 1.0),
        "ctx": nrm(ks[2], (BATCH, CTX_LEN, D_MODEL), 1.0),
        "c_ctx": nrm(ks[3], (D_MODEL,), 1.0),
        "w_ada": nrm(ks[4], (DEPTH, D_MODEL, 6 * D_MODEL), 0.5 * D_MODEL ** -0.5),
        "b_ada": nrm(ks[5], (DEPTH, 6 * D_MODEL), 0.02),
        "w_in": nrm(ks[6], (DEPTH, D_MODEL, MIX_WIDTH), D_MODEL ** -0.5),
        "w_pool": nrm(ks[7], (DEPTH, N_POOL_GROUPS, POOL_GROUP, POOL_GROUP), POOL_GROUP ** -0.5),
        "pool_scale": 1.0 + nrm(ks[8], (DEPTH, POOL_WIDTH), 0.02),
        "ssm_a_re": -0.5 + nrm(ks[9], (DEPTH, 2, G, P), 0.01),
        "ssm_a_im": a_im_base + nrm(ks[10], (DEPTH, 2, G, P), 0.01),
        "ssm_log_dt": jax.random.uniform(ks[11], (DEPTH, 2, G), f32, math.log(DT_MIN), math.log(DT_MAX)),
        "ssm_b_re": nrm(ks[12], (DEPTH, 2, G, P, H), (2 * H) ** -0.5),
        "ssm_b_im": nrm(ks[13], (DEPTH, 2, G, P, H), (2 * H) ** -0.5),
        "ssm_c_re": nrm(ks[14], (DEPTH, 2, G, H, P), P ** -0.5),
        "ssm_c_im": nrm(ks[15], (DEPTH, 2, G, H, P), P ** -0.5),
        "ssm_d": nrm(ks[16], (DEPTH, SSM_WIDTH), 1.0),
        "w_glu": nrm(ks[17], (DEPTH, SSM_WIDTH, SSM_WIDTH), SSM_WIDTH ** -0.5),
        "w_out": nrm(ks[18], (DEPTH, MIX_WIDTH, D_MODEL), MIX_WIDTH ** -0.5),
        "g_pre_mix": 1.0 + nrm(ks[19], (DEPTH, D_MODEL), 0.02),
        "g_post_mix": 1.0 + nrm(ks[20], (DEPTH, D_MODEL), 0.02),
        "g_pre_ffn": 1.0 + nrm(ks[21], (DEPTH, D_MODEL), 0.02),
        "g_post_ffn": 1.0 + nrm(ks[22], (DEPTH, D_MODEL), 0.02),
        "w_up": nrm(ks[23], (DEPTH, D_MODEL, 2 * D_FF), D_MODEL ** -0.5),
        "w_conv": nrm(ks[24], (DEPTH, CONV_K, CONV_K, 2 * D_FF), 1.0 / CONV_K),
        "w_down": nrm(ks[25], (DEPTH, D_FF, D_MODEL), D_FF ** -0.5),
    }


def reference(x, c, ctx, c_ctx, w_ada, b_ada, w_in, w_pool, pool_scale, ssm_a_re, ssm_a_im,
              ssm_log_dt, ssm_b_re, ssm_b_im, ssm_c_re, ssm_c_im, ssm_d, w_glu, w_out,
              g_pre_mix, g_post_mix, g_pre_ffn, g_post_ffn, w_up, w_conv, w_down):
    n_lat = x.shape[1]
    rows = n_lat // GRID_W
    s_c = jax.nn.silu(c)
    s_ctx = jax.nn.silu(c_ctx)[None, :]
    for l in range(DEPTH):
        last = l == DEPTH - 1
        mx = jnp.split(s_c @ w_ada[l] + b_ada[l], 6, axis=-1)
        mc = jnp.split(s_ctx @ w_ada[l] + b_ada[l], 6, axis=-1)
        disc = [s5_discretise(ssm_a_re[l, d], ssm_a_im[l, d], ssm_log_dt[l, d],
                              ssm_b_re[l, d], ssm_b_im[l, d]) for d in range(2)]

        h_ctx = modulate(rms_norm(ctx, g_pre_mix[l]), mc[0], mc[1])
        h_lat = modulate(rms_norm(x, g_pre_mix[l]), mx[0], mx[1])
        u_lat = h_lat @ w_in[l]
        if last:
            u_ctx_ssm = h_ctx @ w_in[l][:, POOL_WIDTH:]
        else:
            u_ctx = h_ctx @ w_in[l]
            u_ctx_ssm = u_ctx[..., POOL_WIDTH:]
        ctx_g = to_ssm_groups(u_ctx_ssm)
        lat_g = to_ssm_groups(u_lat[..., POOL_WIDTH:])
        lat_dirs, ctx_dirs = [], []
        for d in range(2):
            rev = d == 1
            lam_re, lam_im, bb_re, bb_im = disc[d]
            hc = s5_scan(ctx_g, lam_re, lam_im, bb_re, bb_im, None, rev)
            fin = 0 if rev else -1
            h0 = (hc[0][:, fin], hc[1][:, fin])
            hl = s5_scan(lat_g, lam_re, lam_im, bb_re, bb_im, h0, rev)
            lat_dirs.append(s5_readout(hl, ssm_c_re[l, d], ssm_c_im[l, d]))
            if not last:
                ctx_dirs.append(s5_readout(hc, ssm_c_re[l, d], ssm_c_im[l, d]))
        mix_lat = mix_project(u_lat, lat_dirs[0] + lat_dirs[1], w_pool[l], pool_scale[l],
                              ssm_d[l], w_glu[l], w_out[l])
        x = x + mx[2][:, None, :] * rms_norm(mix_lat, g_post_mix[l])

        f_lat = conv_glu_ffn(modulate(rms_norm(x, g_pre_ffn[l]), mx[3], mx[4]),
                             w_up[l], w_conv[l], w_down[l], rows)
        x = x + mx[5][:, None, :] * rms_norm(f_lat, g_post_ffn[l])

        if not last:
            mix_ctx = mix_project(u_ctx, ctx_dirs[0] + ctx_dirs[1], w_pool[l], pool_scale[l],
                                  ssm_d[l], w_glu[l], w_out[l])
            ctx = ctx + mc[2][:, None, :] * rms_norm(mix_ctx, g_post_mix[l])
            f_ctx = conv_glu_ffn(modulate(rms_norm(ctx, g_pre_ffn[l]), mc[3], mc[4]),
                                 w_up[l], w_conv[l], w_down[l], None)
            ctx = ctx + mc[5][:, None, :] * rms_norm(f_ctx, g_post_ffn[l])
    return x
```

```python
import functools

import jax
import jax.numpy as jnp
from jax import lax
from jax.experimental import pallas as pl
from jax.experimental.pallas import tpu as pltpu

GRID_W = 64
POOL_WINDOWS = (2, 4, 8, 16)
POOL_HALO = max(POOL_WINDOWS) // 2
EPS = 1e-6
SUBLANES = 8
LANES = 128
SSM_CHUNK = 8
VMEM_LIMIT_BYTES = 60000 * 1024

F32 = jnp.float32
BF16 = jnp.bfloat16


def _params(*sem):
    return pltpu.CompilerParams(dimension_semantics=sem, vmem_limit_bytes=VMEM_LIMIT_BYTES)


def _const_spec(shape):
    zeros = (0,) * len(shape)
    return pl.BlockSpec(shape, lambda *_: zeros, pipeline_mode=pl.Buffered(1))


def _rms(v, gain):
    ms = jnp.mean(v * v, axis=-1, keepdims=True)
    return v * lax.rsqrt(ms + EPS) * gain


def _ada_kernel(c_ref, w_ref, b_ref, o_ref):
    s = jax.nn.silu(c_ref[...]).astype(BF16)
    r = jnp.dot(s, w_ref[...].astype(BF16), preferred_element_type=F32) + b_ref[...]
    o_ref[...] = r.reshape(o_ref.shape)


def _ada_all(c16, w_ada, b_ada, nb):
    depth, d, n = w_ada.shape
    return pl.pallas_call(
        _ada_kernel,
        grid=(depth, n // nb),
        in_specs=[
            pl.BlockSpec((2 * SUBLANES, d), lambda l, j: (0, 0)),
            pl.BlockSpec((None, d, nb), lambda l, j: (l, 0, j)),
            pl.BlockSpec((None, 1, nb), lambda l, j: (l, 0, j)),
        ],
        out_specs=pl.BlockSpec((None, 2, SUBLANES, nb), lambda l, j: (l, 0, 0, j)),
        out_shape=jax.ShapeDtypeStruct((depth, 2, SUBLANES, n), F32),
        compiler_params=_params("arbitrary", "arbitrary"),
        name="ada",
    )(c16, w_ada, b_ada.reshape(depth, 1, n))


def _premix_kernel(x_ref, sh_ref, sc_ref, g_ref, w_ref, u_ref):
    tt, b, d = x_ref.shape
    h = _rms(x_ref[...], g_ref[...]) * (1.0 + sc_ref[...]) + sh_ref[...]
    u = jnp.dot(h.reshape(tt * b, d).astype(BF16), w_ref[...], preferred_element_type=F32)
    u_ref[...] = u.reshape(u_ref.shape)


def _premix(xs, mod, g, w, tt, n_lat_tiles):
    s, b, d = xs.shape
    n = w.shape[1]
    sel = lambda i: jnp.where(i >= n_lat_tiles, 1, 0)
    return pl.pallas_call(
        _premix_kernel,
        grid=(s // tt,),
        in_specs=[
            pl.BlockSpec((tt, b, d), lambda i: (i, 0, 0)),
            pl.BlockSpec((None, b, d), lambda i: (sel(i), 0, 0)),
            pl.BlockSpec((None, b, d), lambda i: (sel(i), 0, 1)),
            _const_spec((1, d)),
            _const_spec((d, n)),
        ],
        out_specs=pl.BlockSpec((tt, b, n), lambda i: (i, 0, 0)),
        out_shape=jax.ShapeDtypeStruct((s, b, n), F32),
        compiler_params=_params("arbitrary"),
        name="premix",
    )(xs, mod, mod, g, w)


def _ssm_kernel(u_ref, mo_ref, my_ref, lam_ref, h0_ref, *rest, chunks, aliased):
    if aliased:
        rest = rest[1:]
    y_ref, hf_ref, s_ref, hin_ref, hc_ref = rest
    c_n = chunks
    sb = hc_ref.shape[-1]
    half = sb // 2
    d = pl.program_id(0)
    t = pl.program_id(2)

    @pl.when(t == 0)
    def _():
        hc_ref[...] = h0_ref[...]

    u_rows = jnp.concatenate(
        [u_ref[:, k].reshape(c_n * SUBLANES, LANES).astype(BF16) for k in range(SSM_CHUNK)], axis=-1)
    s_ref[...] = jnp.dot(u_rows, mo_ref[...], preferred_element_type=F32).reshape(s_ref.shape)

    lre = jnp.broadcast_to(lam_ref[0:1, :], (SUBLANES, half))
    lim = jnp.broadcast_to(lam_ref[1:2, :], (SUBLANES, half))

    def step(i, carry):
        hr, hi = carry
        c = jnp.where(d == 0, i, c_n - 1 - i)
        hin_ref[c, :, :half] = hr
        hin_ref[c, :, half:] = hi
        sr = s_ref[c, :, :half]
        si = s_ref[c, :, half:]
        return lre * hr - lim * hi + sr, lre * hi + lim * hr + si

    hr, hi = lax.fori_loop(0, c_n, step, (hc_ref[:, :half], hc_ref[:, half:]))
    hc_ref[:, :half] = hr
    hc_ref[:, half:] = hi
    hf_ref[:, :half] = hr
    hf_ref[:, half:] = hi

    h_rows = hin_ref[...].reshape(c_n * SUBLANES, sb).astype(BF16)
    y = jnp.dot(jnp.concatenate([u_rows, h_rows], axis=-1), my_ref[...], preferred_element_type=F32)
    for k in range(SSM_CHUNK):
        y_ref[:, k] = y[:, k * LANES:(k + 1) * LANES].reshape(c_n, SUBLANES, LANES)


def _ssm(u5, mats, h0, y_prev, chunks, block_off, n_tiles, ssm_lane_block0):
    m_out, m_y, lam8 = mats
    n_chunks_total, _, b, _ = u5.shape
    _, nb, kk, sb = m_out.shape
    tile = lambda d, t: jnp.where(d == 0, t, n_tiles - 1 - t) + block_off
    in_specs = [
        pl.BlockSpec((chunks, SSM_CHUNK, b, LANES), lambda d, j, t: (tile(d, t), 0, 0, ssm_lane_block0 + j)),
        pl.BlockSpec((None, None, kk, sb), lambda d, j, t: (d, j, 0, 0)),
        pl.BlockSpec((None, None, kk + sb, kk), lambda d, j, t: (d, j, 0, 0)),
        pl.BlockSpec((None, None, 2, sb // 2), lambda d, j, t: (d, j, 0, 0)),
        pl.BlockSpec((None, None, b, sb), lambda d, j, t: (d, j, 0, 0)),
    ]
    args = [u5, m_out, m_y, lam8, h0]
    aliases = {}
    if y_prev is not None:
        in_specs.append(pl.BlockSpec(memory_space=pl.ANY))
        args.append(y_prev)
        aliases = {5: 0}
    y_shape = (2, n_chunks_total, SSM_CHUNK, b, nb * LANES)
    return pl.pallas_call(
        functools.partial(_ssm_kernel, chunks=chunks, aliased=y_prev is not None),
        grid=(2, nb, n_tiles),
        in_specs=in_specs,
        out_specs=[
            pl.BlockSpec((None, chunks, SSM_CHUNK, b, LANES), lambda d, j, t: (d, tile(d, t), 0, 0, j)),
            pl.BlockSpec((None, None, b, sb), lambda d, j, t: (d, j, 0, 0)),
        ],
        out_shape=[jax.ShapeDtypeStruct(y_shape, F32), jax.ShapeDtypeStruct(h0.shape, F32)],
        scratch_shapes=[
            pltpu.VMEM((chunks, b, sb), F32),
            pltpu.VMEM((chunks, b, sb), F32),
            pltpu.VMEM((b, sb), F32),
        ],
        input_output_aliases=aliases,
        compiler_params=_params("arbitrary", "arbitrary", "arbitrary"),
        name="ssm",
    )(*args)


def _ssm_matrices(a_re, a_im, log_dt, b_re, b_im, c_re, c_im, reverse):
    g, p = a_re.shape
    h = b_re.shape[-1]
    t_n = SSM_CHUNK
    gpb = LANES // h
    nb = g // gpb
    a_re = a_re.astype(F32)
    a_im = a_im.astype(F32)
    dt = jnp.exp(log_dt.astype(F32))[:, None]
    mag = jnp.exp(a_re * dt)
    lam_re = mag * jnp.cos(a_im * dt)
    lam_im = mag * jnp.sin(a_im * dt)
    denom = a_re * a_re + a_im * a_im
    nr, ni = lam_re - 1.0, lam_im
    f_re = ((nr * a_re + ni * a_im) / denom)[..., None]
    f_im = ((ni * a_re - nr * a_im) / denom)[..., None]
    b_re = b_re.astype(F32)
    b_im = b_im.astype(F32)
    bb_re = f_re * b_re - f_im * b_im
    bb_im = f_re * b_im + f_im * b_re
    c_re = c_re.astype(F32)
    c_im = c_im.astype(F32)
    pr, pi = [jnp.ones_like(lam_re)], [jnp.zeros_like(lam_im)]
    for _ in range(t_n):
        pr, pi = (pr + [pr[-1] * lam_re - pi[-1] * lam_im], pi + [pr[-1] * lam_im + pi[-1] * lam_re])
    pr = jnp.stack(pr)
    pi = jnp.stack(pi)

    q_re = pr[:t_n, :, :, None] * bb_re - pi[:t_n, :, :, None] * bb_im
    q_im = pr[:t_n, :, :, None] * bb_im + pi[:t_n, :, :, None] * bb_re
    hi = lax.Precision.HIGHEST
    taps = (jnp.einsum('gop,kgpi->kgoi', c_re, q_re, precision=hi)
            - jnp.einsum('gop,kgpi->kgoi', c_im, q_im, precision=hi))
    lag = jnp.arange(t_n)[None, :] - jnp.arange(t_n)[:, None]
    toep = jnp.where((lag >= 0)[:, :, None, None, None], taps[jnp.clip(lag, 0, t_n - 1)], 0.0)
    toep = toep.reshape(t_n, t_n, nb, gpb, h, h)
    cr, ci = c_re[None], c_im[None]
    lr, li = pr[1:, :, None, :], pi[1:, :, None, :]
    w_in = jnp.stack([cr * lr - ci * li, -(cr * li + ci * lr)])
    w_in = w_in.reshape(2, t_n, nb, gpb, h, p)
    w_out = jnp.stack([q_re[::-1], q_im[::-1]]).reshape(2, t_n, nb, gpb, p, h)
    if reverse:
        toep = toep[::-1, ::-1]
        w_in = w_in[:, ::-1]
        w_out = w_out[:, ::-1]
    k_dim = t_n * gpb * h
    sb = 2 * gpb * p
    eye = jnp.eye(gpb, dtype=F32)
    m_toep = (toep.transpose(2, 0, 3, 5, 1, 4)[:, :, :, :, :, None, :]
              * eye[None, None, :, None, None, :, None]).reshape(nb, k_dim, k_dim)
    m_in = (w_in.transpose(2, 0, 3, 5, 1, 4)[:, :, :, :, :, None, :]
            * eye[None, None, :, None, None, :, None]).reshape(nb, sb, k_dim)
    m_out = (w_out.transpose(2, 1, 3, 5, 0, 4)[:, :, :, :, :, None, :]
             * eye[None, None, :, None, None, :, None]).reshape(nb, k_dim, sb)
    m_y = jnp.concatenate([m_toep, m_in], axis=1)
    lam8 = jnp.stack([pr[t_n].reshape(nb, gpb * p), pi[t_n].reshape(nb, gpb * p)], axis=1)
    return m_out.astype(BF16), m_y.astype(BF16), lam8


def _mix_kernel(um_ref, up_ref, un_ref, us_ref, y_ref, x_ref, gate_ref, wp_ref, ps_ref, sd_ref,
                wg_ref, wo_ref, gp_ref, o_ref, ext_ref, cat_ref, *, n_lat_tiles, seq, ctx):
    tt, b, pw_all = um_ref.shape
    pg = pw_all // len(POOL_WINDOWS)
    sw = us_ref.shape[-1]
    rows = tt * b
    i = pl.program_id(0)
    is_ctx = i >= n_lat_tiles
    t0 = jnp.where(is_ctx, i - n_lat_tiles, i) * tt
    seg = jnp.where(is_ctx, ctx, seq)
    has_prev = (t0 > 0).astype(F32)
    has_next = (t0 + tt < seg).astype(F32)
    hl = POOL_HALO
    ext_ref[0:hl] = up_ref[...] * has_prev
    ext_ref[hl:hl + tt] = um_ref[...]
    ext_ref[hl + tt:hl + tt + hl] = un_ref[...] * has_next

    tl = t0 + lax.broadcasted_iota(jnp.int32, (tt, b, pg), 0)
    for g, w in enumerate(POOL_WINDOWS):
        sl = slice(g * pg, (g + 1) * pg)
        lo = hl - w // 2
        acc = ext_ref[lo:lo + tt, :, sl]
        for k in range(1, w):
            acc = acc + ext_ref[lo + k:lo + k + tt, :, sl]
        cnt = (jnp.minimum(tl + w // 2, seg) - jnp.maximum(tl - w // 2, 0)).astype(F32)
        p = acc / cnt - ext_ref[hl:hl + tt, :, sl]
        po = jnp.dot(p.reshape(rows, pg).astype(BF16), wp_ref[g], preferred_element_type=F32)
        cat_ref[:, sl] = (po * ps_ref[:, sl]).astype(BF16)

    yv = (y_ref[0] + y_ref[1]).reshape(rows, sw) + sd_ref[...] * us_ref[...].reshape(rows, sw)
    yv = jax.nn.gelu(yv)
    glu = jax.nn.sigmoid(jnp.dot(yv.astype(BF16), wg_ref[...], preferred_element_type=F32))
    cat_ref[:, pw_all:] = (yv * glu).astype(BF16)

    mix = jnp.dot(cat_ref[...], wo_ref[...], preferred_element_type=F32)
    o_ref[...] = x_ref[...] + gate_ref[...] * _rms(mix, gp_ref[...]).reshape(o_ref.shape)


def _mix(u, y, xs, mod, w_pool, pool_scale, ssm_d, w_glu, w_out, g_post, tt, n_tiles, n_lat_tiles,
         seq, ctx):
    s, b, d = xs.shape
    mixw = u.shape[-1]
    sw = y.shape[-1]
    pw_all = mixw - sw
    hl = POOL_HALO
    hpt = tt // hl
    last_halo = n_tiles * hpt - 1
    sel = lambda i: jnp.where(i >= n_lat_tiles, 1, 0)
    return pl.pallas_call(
        functools.partial(_mix_kernel, n_lat_tiles=n_lat_tiles, seq=seq, ctx=ctx),
        grid=(n_tiles,),
        in_specs=[
            pl.BlockSpec((tt, b, pw_all), lambda i: (i, 0, 0)),
            pl.BlockSpec((hl, b, pw_all), lambda i: (jnp.maximum(i * hpt - 1, 0), 0, 0)),
            pl.BlockSpec((hl, b, pw_all), lambda i: (jnp.minimum((i + 1) * hpt, last_halo), 0, 0)),
            pl.BlockSpec((tt, b, sw), lambda i: (i, 0, pw_all // sw)),
            pl.BlockSpec((2, tt, b, sw), lambda i: (0, i, 0, 0)),
            pl.BlockSpec((tt, b, d), lambda i: (i, 0, 0)),
            pl.BlockSpec((None, b, d), lambda i: (sel(i), 0, 2)),
            _const_spec(w_pool.shape),
            _const_spec((1, pw_all)),
            _const_spec((1, sw)),
            _const_spec(w_glu.shape),
            _const_spec(w_out.shape),
            _const_spec((1, d)),
        ],
        out_specs=pl.BlockSpec((tt, b, d), lambda i: (i, 0, 0)),
        out_shape=jax.ShapeDtypeStruct(xs.shape, F32),
        scratch_shapes=[
            pltpu.VMEM((tt + 2 * hl, b, pw_all), F32),
            pltpu.VMEM((tt * b, mixw), BF16),
        ],
        input_output_aliases={5: 0},
        compiler_params=_params("arbitrary"),
        name="mix",
    )(u, u, u, u, y, xs, mod, w_pool, pool_scale, ssm_d, w_glu, w_out, g_post)


def _ffn_up_kernel(x_ref, sh_ref, sc_ref, g_ref, w_ref, z_ref, h_ref):
    @pl.when(pl.program_id(1) == 0)
    def _():
        h = _rms(x_ref[...], g_ref[...]) * (1.0 + sc_ref[...]) + sh_ref[...]
        h_ref[...] = h.reshape(h_ref.shape).astype(BF16)

    z_ref[...] = jnp.dot(h_ref[...], w_ref[...], preferred_element_type=F32).astype(BF16)


def _ffn_up(xs, mod, g, w_up, tt, nb, n_tiles, n_lat_tiles):
    s, b, d = xs.shape
    n = w_up.shape[1]
    sel = lambda i: jnp.where(i >= n_lat_tiles, 1, 0)
    return pl.pallas_call(
        _ffn_up_kernel,
        grid=(n_tiles, n // nb),
        in_specs=[
            pl.BlockSpec((tt, b, d), lambda i, j: (i, 0, 0)),
            pl.BlockSpec((None, b, d), lambda i, j: (sel(i), 0, 3)),
            pl.BlockSpec((None, b, d), lambda i, j: (sel(i), 0, 4)),
            _const_spec((1, d)),
            pl.BlockSpec((d, nb), lambda i, j: (0, j)),
        ],
        out_specs=pl.BlockSpec((tt * b, nb), lambda i, j: (i, j)),
        out_shape=jax.ShapeDtypeStruct((s * b, n), BF16),
        scratch_shapes=[pltpu.VMEM((tt * b, d), BF16)],
        compiler_params=_params("arbitrary", "arbitrary"),
        name="ffn_up",
    )(xs, mod, mod, g, w_up)


def _ffn_down_kernel(va_ref, vm_ref, vb_ref, ga_ref, gm_ref, gb_ref, kv_ref, kg_ref, wd_ref, x_ref,
                     gate_ref, gp_ref, o_ref, acc_ref, ea_ref, em_ref, eb_ref, *, n_lat_tiles,
                     n_ctx_tiles):
    w, b, fb = ea_ref.shape[0] - 2, ea_ref.shape[1], ea_ref.shape[2]
    i = pl.program_id(0)
    j = pl.program_id(1)
    is_ctx = i >= n_lat_tiles
    is_lat = jnp.logical_not(is_ctx)
    ic = i - n_lat_tiles
    vert_up = jnp.logical_and(is_lat, i > 0).astype(F32)
    vert_dn = jnp.logical_and(is_lat, i < n_lat_tiles - 1).astype(F32)
    hz_prev = jnp.logical_and(is_ctx, ic > 0).astype(F32)
    hz_next = jnp.logical_and(is_ctx, ic < n_ctx_tiles - 1).astype(F32)
    zero = jnp.zeros((1, b, fb), F32)

    def conv(a_ref, m_ref, b_ref, k_ref):
        above = a_ref[...].astype(F32).reshape(w, b, fb)
        below = b_ref[...].astype(F32).reshape(w, b, fb)
        ea_ref[0:1] = zero
        ea_ref[1:w + 1] = above
        ea_ref[w + 1:w + 2] = zero
        em_ref[0:1] = above[w - 1:w] * hz_prev
        em_ref[1:w + 1] = m_ref[...].astype(F32).reshape(w, b, fb)
        em_ref[w + 1:w + 2] = below[0:1] * hz_next
        eb_ref[0:1] = zero
        eb_ref[1:w + 1] = below
        eb_ref[w + 1:w + 2] = zero
        out = None
        for dc in range(3):
            term = (ea_ref[dc:dc + w] * (k_ref[dc:dc + 1, :] * vert_up)
                    + em_ref[dc:dc + w] * k_ref[3 + dc:4 + dc, :]
                    + eb_ref[dc:dc + w] * (k_ref[6 + dc:7 + dc, :] * vert_dn))
            out = term if out is None else out + term
        return out

    val = conv(va_ref, vm_ref, vb_ref, kv_ref)
    gat = conv(ga_ref, gm_ref, gb_ref, kg_ref)
    act = (val * jax.nn.silu(gat)).reshape(w * b, fb).astype(BF16)
    part = jnp.dot(act, wd_ref[...], preferred_element_type=F32)

    @pl.when(j == 0)
    def _():
        acc_ref[...] = part

    @pl.when(j > 0)
    def _():
        acc_ref[...] += part

    @pl.when(j == pl.num_programs(1) - 1)
    def _():
        o_ref[...] = x_ref[...] + gate_ref[...] * _rms(acc_ref[...], gp_ref[...]).reshape(o_ref.shape)


def _ffn_down(z, w_conv9, w_down, xs, mod, g_post, fb, n_tiles, n_lat_tiles, n_ctx_tiles):
    s, b, d = xs.shape
    f = w_down.shape[0]
    nf = f // fb
    tt = GRID_W
    rows = tt * b
    up = lambda i: jnp.maximum(i - 1, 0)
    dn = lambda i: jnp.minimum(i + 1, n_tiles - 1)
    sel = lambda i: jnp.where(i >= n_lat_tiles, 1, 0)
    zspec = lambda row, col0: pl.BlockSpec((rows, fb), lambda i, j: (row(i), col0 + j))
    same = lambda i: i
    return pl.pallas_call(
        functools.partial(_ffn_down_kernel, n_lat_tiles=n_lat_tiles, n_ctx_tiles=n_ctx_tiles),
        grid=(n_tiles, nf),
        in_specs=[
            zspec(up, 0), zspec(same, 0), zspec(dn, 0),
            zspec(up, nf), zspec(same, nf), zspec(dn, nf),
            pl.BlockSpec((9, fb), lambda i, j: (0, j)),
            pl.BlockSpec((9, fb), lambda i, j: (0, nf + j)),
            pl.BlockSpec((fb, d), lambda i, j: (j, 0)),
            pl.BlockSpec((tt, b, d), lambda i, j: (i, 0, 0)),
            pl.BlockSpec((None, b, d), lambda i, j: (sel(i), 0, 5)),
            _const_spec((1, d)),
        ],
        out_specs=pl.BlockSpec((tt, b, d), lambda i, j: (i, 0, 0)),
        out_shape=jax.ShapeDtypeStruct(xs.shape, F32),
        scratch_shapes=[
            pltpu.VMEM((rows, d), F32),
            pltpu.VMEM((tt + 2, b, fb), F32),
            pltpu.VMEM((tt + 2, b, fb), F32),
            pltpu.VMEM((tt + 2, b, fb), F32),
        ],
        input_output_aliases={9: 0},
        compiler_params=_params("arbitrary", "arbitrary"),
        name="ffn_down",
    )(z, z, z, z, z, z, w_conv9, w_conv9, w_down, xs, mod, g_post)


def _pick(limit, total):
    t = min(limit, total)
    while total % t:
        t -= 1
    return t


def kernel(x, c, ctx, c_ctx, w_ada, b_ada, w_in, w_pool, pool_scale, ssm_a_re, ssm_a_im, ssm_log_dt,
           ssm_b_re, ssm_b_im, ssm_c_re, ssm_c_im, ssm_d, w_glu, w_out, g_pre_mix, g_post_mix,
           g_pre_ffn, g_post_ffn, w_up, w_conv, w_down):
    bsz, seq, d = x.shape
    n_ctx = ctx.shape[1]
    depth = w_ada.shape[0]
    mixw = w_in.shape[-1]
    sw = ssm_d.shape[-1]
    f = w_down.shape[1]
    assert bsz == SUBLANES and seq % GRID_W == 0 and n_ctx % GRID_W == 0
    assert sw % LANES == 0 and (mixw - sw) % sw == 0 and LANES % ssm_b_re.shape[-1] == 0
    s_all = seq + n_ctx
    tt = GRID_W
    n_lat_tiles = seq // tt
    n_ctx_tiles = n_ctx // tt
    n_tiles = n_lat_tiles + n_ctx_tiles
    tt_up = _pick(2 * GRID_W, n_ctx)
    assert seq % tt_up == 0
    nb_up = _pick(1024, 2 * f)
    fb = _pick(512, f)
    lat_chunks = _pick(64, seq // SSM_CHUNK)
    ctx_chunks = _pick(64, n_ctx // SSM_CHUNK)
    assert (seq // SSM_CHUNK) % ctx_chunks == 0

    xs = jnp.concatenate([x, ctx], axis=1).transpose(1, 0, 2)
    c16 = jnp.concatenate([c, jnp.broadcast_to(c_ctx[None, :], (SUBLANES, d))], axis=0)
    mod_all = _ada_all(c16, w_ada, b_ada, _pick(1024, w_ada.shape[-1]))

    row = lambda v: v.reshape(1, -1).astype(F32)
    for l in range(depth):
        last = l == depth - 1
        mod = mod_all[l]
        mats = [_ssm_matrices(ssm_a_re[l, dr], ssm_a_im[l, dr], ssm_log_dt[l, dr], ssm_b_re[l, dr],
                              ssm_b_im[l, dr], ssm_c_re[l, dr], ssm_c_im[l, dr], dr == 1)
                for dr in range(2)]
        mats = tuple(jnp.stack(m) for m in zip(*mats))

        u = _premix(xs, mod, row(g_pre_mix[l]), w_in[l].astype(BF16), tt, n_lat_tiles)
        u5 = u.reshape(s_all // SSM_CHUNK, SSM_CHUNK, bsz, mixw)
        nbk = sw // LANES
        sb = mats[0].shape[-1]
        h0 = jnp.zeros((2, nbk, bsz, sb), F32)
        lane0 = (mixw - sw) // LANES
        y5, h_ctx = _ssm(u5, mats, h0, None, ctx_chunks, (seq // SSM_CHUNK) // ctx_chunks,
                         (n_ctx // SSM_CHUNK) // ctx_chunks, lane0)
        y5, _ = _ssm(u5, mats, h_ctx, y5, lat_chunks, 0, (seq // SSM_CHUNK) // lat_chunks, lane0)
        y = y5.reshape(2, s_all, bsz, sw)

        nt = n_lat_tiles if last else n_tiles
        xs = _mix(u, y, xs, mod, w_pool[l].astype(BF16), row(pool_scale[l]), row(ssm_d[l]),
                  w_glu[l].astype(BF16), w_out[l].astype(BF16), row(g_post_mix[l]), tt, nt,
                  n_lat_tiles, seq, n_ctx)
        nt_up = (seq if last else s_all) // tt_up
        z = _ffn_up(xs, mod, row(g_pre_ffn[l]), w_up[l].astype(BF16), tt_up, nb_up, nt_up,
                    seq // tt_up)
        xs = _ffn_down(z, w_conv[l].reshape(9, 2 * f).astype(F32), w_down[l].astype(BF16), xs, mod,
                       row(g_post_ffn[l]), fb, nt, n_lat_tiles, n_ctx_tiles)
    return xs[:seq].transpose(1, 0, 2)
```

```python
import functools

import jax
import jax.numpy as jnp
from jax import lax
from jax.experimental import pallas as pl
from jax.experimental.pallas import tpu as pltpu

GRID_W = 64
POOL_WINDOWS = (2, 4, 8, 16)
POOL_HALO = max(POOL_WINDOWS) // 2
EPS = 1e-6
SUBLANES = 8
LANES = 128
SSM_CHUNK = 8
VMEM_LIMIT_BYTES = 60000 * 1024

F32 = jnp.float32
BF16 = jnp.bfloat16


def _params(*sem):
    return pltpu.CompilerParams(dimension_semantics=sem, vmem_limit_bytes=VMEM_LIMIT_BYTES)


def _const_spec(shape):
    zeros = (0,) * len(shape)
    return pl.BlockSpec(shape, lambda *_: zeros, pipeline_mode=pl.Buffered(1))


def _rms(v, gain):
    ms = jnp.mean(v * v, axis=-1, keepdims=True)
    return v * lax.rsqrt(ms + EPS) * gain


def _ada_kernel(c_ref, w_ref, b_ref, o_ref):
    s = jax.nn.silu(c_ref[...]).astype(BF16)
    r = jnp.dot(s, w_ref[...].astype(BF16), preferred_element_type=F32) + b_ref[...]
    o_ref[...] = r.reshape(o_ref.shape)


def _ada_all(c16, w_ada, b_ada, nb):
    depth, d, n = w_ada.shape
    return pl.pallas_call(
        _ada_kernel,
        grid=(depth, n // nb),
        in_specs=[
            pl.BlockSpec((2 * SUBLANES, d), lambda l, j: (0, 0)),
            pl.BlockSpec((None, d, nb), lambda l, j: (l, 0, j)),
            pl.BlockSpec((None, 1, nb), lambda l, j: (l, 0, j)),
        ],
        out_specs=pl.BlockSpec((None, 2, SUBLANES, nb), lambda l, j: (l, 0, 0, j)),
        out_shape=jax.ShapeDtypeStruct((depth, 2, SUBLANES, n), F32),
        compiler_params=_params("arbitrary", "arbitrary"),
        name="ada",
    )(c16, w_ada, b_ada.reshape(depth, 1, n))


def _premix_kernel(x_ref, sh_ref, sc_ref, g_ref, w_ref, u_ref):
    tt, b, d = x_ref.shape
    h = _rms(x_ref[...], g_ref[...]) * (1.0 + sc_ref[...]) + sh_ref[...]
    u = jnp.dot(h.reshape(tt * b, d).astype(BF16), w_ref[...], preferred_element_type=F32)
    u_ref[...] = u.reshape(u_ref.shape)


def _premix(xs, mod, g, w, tt, n_lat_tiles):
    s, b, d = xs.shape
    n = w.shape[1]
    sel = lambda i: jnp.where(i >= n_lat_tiles, 1, 0)
    return pl.pallas_call(
        _premix_kernel,
        grid=(s // tt,),
        in_specs=[
            pl.BlockSpec((tt, b, d), lambda i: (i, 0, 0)),
            pl.BlockSpec((None, b, d), lambda i: (sel(i), 0, 0)),
            pl.BlockSpec((None, b, d), lambda i: (sel(i), 0, 1)),
            _const_spec((1, d)),
            _const_spec((d, n)),
        ],
        out_specs=pl.BlockSpec((tt, b, n), lambda i: (i, 0, 0)),
        out_shape=jax.ShapeDtypeStruct((s, b, n), F32),
        compiler_params=_params("arbitrary"),
        name="premix",
    )(xs, mod, mod, g, w)


def _ssm_kernel(u_ref, mo_ref, my_ref, lam_ref, h0_ref, *rest, chunks, aliased):
    if aliased:
        rest = rest[1:]
    y_ref, hf_ref, s_ref, hin_ref, hc_ref = rest
    c_n = chunks
    sb = hc_ref.shape[-1]
    half = sb // 2
    d = pl.program_id(0)
    t = pl.program_id(2)

    @pl.when(t == 0)
    def _():
        hc_ref[...] = h0_ref[...]

    u_rows = jnp.concatenate(
        [u_ref[:, k].reshape(c_n * SUBLANES, LANES).astype(BF16) for k in range(SSM_CHUNK)], axis=-1)
    s_ref[...] = jnp.dot(u_rows, mo_ref[...], preferred_element_type=F32).reshape(s_ref.shape)

    lre = jnp.broadcast_to(lam_ref[0:1, :], (SUBLANES, half))
    lim = jnp.broadcast_to(lam_ref[1:2, :], (SUBLANES, half))

    def step(i, carry):
        hr, hi = carry
        c = jnp.where(d == 0, i, c_n - 1 - i)
        hin_ref[c, :, :half] = hr
        hin_ref[c, :, half:] = hi
        sr = s_ref[c, :, :half]
        si = s_ref[c, :, half:]
        return lre * hr - lim * hi + sr, lre * hi + lim * hr + si

    hr, hi = lax.fori_loop(0, c_n, step, (hc_ref[:, :half], hc_ref[:, half:]))
    hc_ref[:, :half] = hr
    hc_ref[:, half:] = hi
    hf_ref[:, :half] = hr
    hf_ref[:, half:] = hi

    h_rows = hin_ref[...].reshape(c_n * SUBLANES, sb).astype(BF16)
    y = jnp.dot(jnp.concatenate([u_rows, h_rows], axis=-1), my_ref[...], preferred_element_type=F32)
    for k in range(SSM_CHUNK):
        y_ref[:, k] = y[:, k * LANES:(k + 1) * LANES].reshape(c_n, SUBLANES, LANES)


def _ssm(u5, mats, h0, y_prev, chunks, block_off, n_tiles, ssm_lane_block0):
    m_out, m_y, lam8 = mats
    n_chunks_total, _, b, _ = u5.shape
    _, nb, kk, sb = m_out.shape
    tile = lambda d, t: jnp.where(d == 0, t, n_tiles - 1 - t) + block_off
    in_specs = [
        pl.BlockSpec((chunks, SSM_CHUNK, b, LANES), lambda d, j, t: (tile(d, t), 0, 0, ssm_lane_block0 + j)),
        pl.BlockSpec((None, None, kk, sb), lambda d, j, t: (d, j, 0, 0)),
        pl.BlockSpec((None, None, kk + sb, kk), lambda d, j, t: (d, j, 0, 0)),
        pl.BlockSpec((None, None, 2, sb // 2), lambda d, j, t: (d, j, 0, 0)),
        pl.BlockSpec((None, None, b, sb), lambda d, j, t: (d, j, 0, 0)),
    ]
    args = [u5, m_out, m_y, lam8, h0]
    aliases = {}
    if y_prev is not None:
        in_specs.append(pl.BlockSpec(memory_space=pl.ANY))
        args.append(y_prev)
        aliases = {5: 0}
    y_shape = (2, n_chunks_total, SSM_CHUNK, b, nb * LANES)
    return pl.pallas_call(
        functools.partial(_ssm_kernel, chunks=chunks, aliased=y_prev is not None),
        grid=(2, nb, n_tiles),
        in_specs=in_specs,
        out_specs=[
            pl.BlockSpec((None, chunks, SSM_CHUNK, b, LANES), lambda d, j, t: (d, tile(d, t), 0, 0, j)),
            pl.BlockSpec((None, None, b, sb), lambda d, j, t: (d, j, 0, 0)),
        ],
        out_shape=[jax.ShapeDtypeStruct(y_shape, F32), jax.ShapeDtypeStruct(h0.shape, F32)],
        scratch_shapes=[
            pltpu.VMEM((chunks, b, sb), F32),
            pltpu.VMEM((chunks, b, sb), F32),
            pltpu.VMEM((b, sb), F32),
        ],
        input_output_aliases=aliases,
        compiler_params=_params("arbitrary", "arbitrary", "arbitrary"),
        name="ssm",
    )(*args)


def _ssm_matrices(a_re, a_im, log_dt, b_re, b_im, c_re, c_im, reverse):
    g, p = a_re.shape
    h = b_re.shape[-1]
    t_n = SSM_CHUNK
    gpb = LANES // h
    nb = g // gpb
    a_re = a_re.astype(F32)
    a_im = a_im.astype(F32)
    dt = jnp.exp(log_dt.astype(F32))[:, None]
    mag = jnp.exp(a_re * dt)
    lam_re = mag * jnp.cos(a_im * dt)
    lam_im = mag * jnp.sin(a_im * dt)
    denom = a_re * a_re + a_im * a_im
    nr, ni = lam_re - 1.0, lam_im
    f_re = ((nr * a_re + ni * a_im) / denom)[..., None]
    f_im = ((ni * a_re - nr * a_im) / denom)[..., None]
    b_re = b_re.astype(F32)
    b_im = b_im.astype(F32)
    bb_re = f_re * b_re - f_im * b_im
    bb_im = f_re * b_im + f_im * b_re
    c_re = c_re.astype(F32)
    c_im = c_im.astype(F32)
    pr, pi = [jnp.ones_like(lam_re)], [jnp.zeros_like(lam_im)]
    for _ in range(t_n):
        pr, pi = (pr + [pr[-1] * lam_re - pi[-1] * lam_im], pi + [pr[-1] * lam_im + pi[-1] * lam_re])
    pr = jnp.stack(pr)
    pi = jnp.stack(pi)

    q_re = pr[:t_n, :, :, None] * bb_re - pi[:t_n, :, :, None] * bb_im
    q_im = pr[:t_n, :, :, None] * bb_im + pi[:t_n, :, :, None] * bb_re
    hi = lax.Precision.HIGHEST
    taps = (jnp.einsum('gop,kgpi->kgoi', c_re, q_re, precision=hi)
            - jnp.einsum('gop,kgpi->kgoi', c_im, q_im, precision=hi))
    lag = jnp.arange(t_n)[None, :] - jnp.arange(t_n)[:, None]
    toep = jnp.where((lag >= 0)[:, :, None, None, None], taps[jnp.clip(lag, 0, t_n - 1)], 0.0)
    toep = toep.reshape(t_n, t_n, nb, gpb, h, h)
    cr, ci = c_re[None], c_im[None]
    lr, li = pr[1:, :, None, :], pi[1:, :, None, :]
    w_in = jnp.stack([cr * lr - ci * li, -(cr * li + ci * lr)])
    w_in = w_in.reshape(2, t_n, nb, gpb, h, p)
    w_out = jnp.stack([q_re[::-1], q_im[::-1]]).reshape(2, t_n, nb, gpb, p, h)
    if reverse:
        toep = toep[::-1, ::-1]
        w_in = w_in[:, ::-1]
        w_out = w_out[:, ::-1]
    k_dim = t_n * gpb * h
    sb = 2 * gpb * p

    def on_group_diagonal(compact, row_width, col_width, n_cols):
        n_rows, n_compact = compact.shape[1:]
        col = jnp.arange(n_cols)
        src = (col // (gpb * col_width)) * col_width + col % col_width
        place = (jnp.arange(n_compact)[:, None] == src[None, :]).astype(F32)
        full = jnp.einsum('jrk,kc->jrc', compact, place)
        row_group = (jnp.arange(n_rows) // row_width) % gpb
        col_group = (col // col_width) % gpb
        return jnp.where(row_group[:, None] == col_group[None, :], full, 0.0).astype(BF16)

    m_toep = on_group_diagonal(toep.transpose(2, 0, 3, 5, 1, 4).reshape(nb, k_dim, t_n * h), h, h, k_dim)
    m_in = on_group_diagonal(w_in.transpose(2, 0, 3, 5, 1, 4).reshape(nb, sb, t_n * h), p, h, k_dim)
    m_out = on_group_diagonal(w_out.transpose(2, 1, 3, 5, 0, 4).reshape(nb, k_dim, 2 * p), h, p, sb)
    m_y = jnp.concatenate([m_toep, m_in], axis=1)
    lam8 = jnp.stack([pr[t_n].reshape(nb, gpb * p), pi[t_n].reshape(nb, gpb * p)], axis=1)
    return m_out, m_y, lam8


def _mix_kernel(um_ref, up_ref, un_ref, us_ref, y_ref, x_ref, gate_ref, wp_ref, ps_ref, sd_ref,
                wg_ref, wo_ref, gp_ref, o_ref, ext_ref, cat_ref, *, n_lat_tiles, seq, ctx):
    tt, b, pw_all = um_ref.shape
    pg = pw_all // len(POOL_WINDOWS)
    sw = us_ref.shape[-1]
    rows = tt * b
    i = pl.program_id(0)
    is_ctx = i >= n_lat_tiles
    t0 = jnp.where(is_ctx, i - n_lat_tiles, i) * tt
    seg = jnp.where(is_ctx, ctx, seq)
    has_prev = (t0 > 0).astype(F32)
    has_next = (t0 + tt < seg).astype(F32)
    hl = POOL_HALO
    ext_ref[0:hl] = up_ref[...] * has_prev
    ext_ref[hl:hl + tt] = um_ref[...]
    ext_ref[hl + tt:hl + tt + hl] = un_ref[...] * has_next

    tl = t0 + lax.broadcasted_iota(jnp.int32, (tt, b, pg), 0)
    for g, w in enumerate(POOL_WINDOWS):
        sl = slice(g * pg, (g + 1) * pg)
        lo = hl - w // 2
        acc = ext_ref[lo:lo + tt, :, sl]
        for k in range(1, w):
            acc = acc + ext_ref[lo + k:lo + k + tt, :, sl]
        cnt = (jnp.minimum(tl + w // 2, seg) - jnp.maximum(tl - w // 2, 0)).astype(F32)
        p = acc / cnt - ext_ref[hl:hl + tt, :, sl]
        po = jnp.dot(p.reshape(rows, pg).astype(BF16), wp_ref[g], preferred_element_type=F32)
        cat_ref[:, sl] = (po * ps_ref[:, sl]).astype(BF16)

    yv = (y_ref[0] + y_ref[1]).reshape(rows, sw) + sd_ref[...] * us_ref[...].reshape(rows, sw)
    yv = jax.nn.gelu(yv)
    glu = jax.nn.sigmoid(jnp.dot(yv.astype(BF16), wg_ref[...], preferred_element_type=F32))
    cat_ref[:, pw_all:] = (yv * glu).astype(BF16)

    mix = jnp.dot(cat_ref[...], wo_ref[...], preferred_element_type=F32)
    o_ref[...] = x_ref[...] + gate_ref[...] * _rms(mix, gp_ref[...]).reshape(o_ref.shape)


def _mix(u, y, xs, mod, w_pool, pool_scale, ssm_d, w_glu, w_out, g_post, tt, n_tiles, n_lat_tiles,
         seq, ctx):
    s, b, d = xs.shape
    mixw = u.shape[-1]
    sw = y.shape[-1]
    pw_all = mixw - sw
    hl = POOL_HALO
    hpt = tt // hl
    last_halo = n_tiles * hpt - 1
    sel = lambda i: jnp.where(i >= n_lat_tiles, 1, 0)
    return pl.pallas_call(
        functools.partial(_mix_kernel, n_lat_tiles=n_lat_tiles, seq=seq, ctx=ctx),
        grid=(n_tiles,),
        in_specs=[
            pl.BlockSpec((tt, b, pw_all), lambda i: (i, 0, 0)),
            pl.BlockSpec((hl, b, pw_all), lambda i: (jnp.maximum(i * hpt - 1, 0), 0, 0)),
            pl.BlockSpec((hl, b, pw_all), lambda i: (jnp.minimum((i + 1) * hpt, last_halo), 0, 0)),
            pl.BlockSpec((tt, b, sw), lambda i: (i, 0, pw_all // sw)),
            pl.BlockSpec((2, tt, b, sw), lambda i: (0, i, 0, 0)),
            pl.BlockSpec((tt, b, d), lambda i: (i, 0, 0)),
            pl.BlockSpec((None, b, d), lambda i: (sel(i), 0, 2)),
            _const_spec(w_pool.shape),
            _const_spec((1, pw_all)),
            _const_spec((1, sw)),
            _const_spec(w_glu.shape),
            _const_spec(w_out.shape),
            _const_spec((1, d)),
        ],
        out_specs=pl.BlockSpec((tt, b, d), lambda i: (i, 0, 0)),
        out_shape=jax.ShapeDtypeStruct(xs.shape, F32),
        scratch_shapes=[
            pltpu.VMEM((tt + 2 * hl, b, pw_all), F32),
            pltpu.VMEM((tt * b, mixw), BF16),
        ],
        input_output_aliases={5: 0},
        compiler_params=_params("arbitrary"),
        name="mix",
    )(u, u, u, u, y, xs, mod, w_pool, pool_scale, ssm_d, w_glu, w_out, g_post)


def _ffn_up_kernel(x_ref, sh_ref, sc_ref, g_ref, w_ref, z_ref, h_ref):
    @pl.when(pl.program_id(1) == 0)
    def _():
        h = _rms(x_ref[...], g_ref[...]) * (1.0 + sc_ref[...]) + sh_ref[...]
        h_ref[...] = h.reshape(h_ref.shape).astype(BF16)

    z_ref[...] = jnp.dot(h_ref[...], w_ref[...], preferred_element_type=F32).astype(BF16)


def _ffn_up(xs, mod, g, w_up, tt, nb, n_tiles, n_lat_tiles):
    s, b, d = xs.shape
    n = w_up.shape[1]
    sel = lambda i: jnp.where(i >= n_lat_tiles, 1, 0)
    return pl.pallas_call(
        _ffn_up_kernel,
        grid=(n_tiles, n // nb),
        in_specs=[
            pl.BlockSpec((tt, b, d), lambda i, j: (i, 0, 0)),
            pl.BlockSpec((None, b, d), lambda i, j: (sel(i), 0, 3)),
            pl.BlockSpec((None, b, d), lambda i, j: (sel(i), 0, 4)),
            _const_spec((1, d)),
            pl.BlockSpec((d, nb), lambda i, j: (0, j)),
        ],
        out_specs=pl.BlockSpec((tt * b, nb), lambda i, j: (i, j)),
        out_shape=jax.ShapeDtypeStruct((s * b, n), BF16),
        scratch_shapes=[pltpu.VMEM((tt * b, d), BF16)],
        compiler_params=_params("arbitrary", "arbitrary"),
        name="ffn_up",
    )(xs, mod, mod, g, w_up)


FFN_QUARTERS = 4


def _ffn_down_kernel(va_ref, vm_ref, vb_ref, ga_ref, gm_ref, gb_ref, kv_ref, kg_ref, wd_ref, x_ref,
                     gate_ref, gp_ref, o_ref, acc_ref, act_ref, kb_ref, eva_ref, evm_ref, evb_ref,
                     ega_ref, egm_ref, egb_ref, *, n_lat_tiles, n_ctx_tiles):
    w, b, fb = evm_ref.shape[0] - 2, evm_ref.shape[1], evm_ref.shape[2]
    i = pl.program_id(0)
    j = pl.program_id(1)
    nf = pl.num_programs(1) - 1
    is_ctx = i >= n_lat_tiles
    is_lat = jnp.logical_not(is_ctx)
    ic = i - n_lat_tiles
    vert_up = jnp.logical_and(is_lat, i > 0).astype(F32)
    vert_dn = jnp.logical_and(is_lat, i < n_lat_tiles - 1).astype(F32)
    hz_prev = jnp.logical_and(is_ctx, ic > 0).astype(F32)
    hz_next = jnp.logical_and(is_ctx, ic < n_ctx_tiles - 1).astype(F32)
    slot = j % 2

    @pl.when(j == 0)
    def _():
        acc_ref[...] = jnp.zeros(acc_ref.shape, F32)
        act_ref[1] = jnp.zeros(act_ref.shape[1:], BF16)

    streams = ((va_ref, vm_ref, vb_ref, eva_ref, evm_ref, evb_ref, kv_ref),
               (ga_ref, gm_ref, gb_ref, ega_ref, egm_ref, egb_ref, kg_ref))
    pair = 2 * b
    zero_row = jnp.zeros((1, b, fb), F32)
    for which, (a_ref, m_ref, b_ref, ea_ref, em_ref, eb_ref, k_ref) in enumerate(streams):
        ea_ref[0:1] = zero_row
        ea_ref[w + 1:w + 2] = zero_row
        eb_ref[0:1] = zero_row
        eb_ref[w + 1:w + 2] = zero_row
        last_above = a_ref[(w - 2) * b:w * b, :].astype(F32)[b:pair]
        first_below = b_ref[0:pair, :].astype(F32)[0:b]
        em_ref[0:1] = (last_above * hz_prev).reshape(1, b, fb)
        em_ref[w + 1:w + 2] = (first_below * hz_next).reshape(1, b, fb)
        for r, flag in enumerate((vert_up, None, vert_dn)):
            for dc in range(3):
                kr = k_ref[3 * r + dc:3 * r + dc + 1, :]
                kr = kr if flag is None else kr * flag
                kb_ref[9 * which + 3 * r + dc] = jnp.broadcast_to(kr, (b, fb))

    tq = w // FFN_QUARTERS
    nq_lanes = acc_ref.shape[-1] // FFN_QUARTERS
    tp = SUBLANES

    def stage(q):
        for a_ref, m_ref, b_ref, ea_ref, em_ref, eb_ref, _ in streams:
            for src, dst in ((a_ref, ea_ref), (m_ref, em_ref), (b_ref, eb_ref)):
                blk = src[q * tq * b:(q + 1) * tq * b, :].astype(F32)
                dst[1 + q * tq:1 + (q + 1) * tq] = blk.reshape(tq, b, fb)

    def conv_piece(t0, lanes, which):
        out = None
        for r, e_ref in enumerate(streams[which][3:6]):
            for dc in range(3):
                term = e_ref[t0 + dc:t0 + dc + tp, :, lanes] * kb_ref[9 * which + 3 * r + dc, :, lanes]
                out = term if out is None else out + term
        return out

    stage(0)
    for q in range(FFN_QUARTERS):
        if q + 1 < FFN_QUARTERS:
            stage(q + 1)
        for lc in range(fb // LANES):
            lanes = slice(lc * LANES, (lc + 1) * LANES)
            for t0 in range(q * tq, (q + 1) * tq, tp):
                val = conv_piece(t0, lanes, 0)
                gat = conv_piece(t0, lanes, 1)
                act = (val * jax.nn.silu(gat)).reshape(tp * b, LANES).astype(BF16)
                act_ref[slot, t0 * b:(t0 + tp) * b, lanes] = act
        cols = slice(q * nq_lanes, (q + 1) * nq_lanes)
        acc_ref[:, cols] += jnp.dot(act_ref[1 - slot], wd_ref[:, cols], preferred_element_type=F32)

    @pl.when(j == nf)
    def _():
        o_ref[...] = x_ref[...] + gate_ref[...] * _rms(acc_ref[...], gp_ref[...]).reshape(o_ref.shape)


def _ffn_down(z, w_conv9, w_down, xs, mod, g_post, fb, n_tiles, n_lat_tiles, n_ctx_tiles):
    s, b, d = xs.shape
    f = w_down.shape[0]
    nf = f // fb
    tt = GRID_W
    rows = tt * b
    up = lambda i: jnp.maximum(i - 1, 0)
    dn = lambda i: jnp.minimum(i + 1, n_tiles - 1)
    sel = lambda i: jnp.where(i >= n_lat_tiles, 1, 0)
    cur = lambda j: jnp.minimum(j, nf - 1)
    prv = lambda j: jnp.maximum(j - 1, 0)
    zspec = lambda row, col0: pl.BlockSpec((rows, fb), lambda i, j: (row(i), col0 + cur(j)))
    same = lambda i: i
    assert tt % (FFN_QUARTERS * SUBLANES) == 0 and d % (FFN_QUARTERS * LANES) == 0
    return pl.pallas_call(
        functools.partial(_ffn_down_kernel, n_lat_tiles=n_lat_tiles, n_ctx_tiles=n_ctx_tiles),
        grid=(n_tiles, nf + 1),
        in_specs=[
            zspec(up, 0), zspec(same, 0), zspec(dn, 0),
            zspec(up, nf), zspec(same, nf), zspec(dn, nf),
            pl.BlockSpec((9, fb), lambda i, j: (0, cur(j))),
            pl.BlockSpec((9, fb), lambda i, j: (0, nf + cur(j))),
            pl.BlockSpec((fb, d), lambda i, j: (prv(j), 0)),
            pl.BlockSpec((tt, b, d), lambda i, j: (i, 0, 0)),
            pl.BlockSpec((None, b, d), lambda i, j: (sel(i), 0, 5)),
            _const_spec((1, d)),
        ],
        out_specs=pl.BlockSpec((tt, b, d), lambda i, j: (i, 0, 0)),
        out_shape=jax.ShapeDtypeStruct(xs.shape, F32),
        scratch_shapes=[
            pltpu.VMEM((rows, d), F32),
            pltpu.VMEM((2, rows, fb), BF16),
            pltpu.VMEM((18, b, fb), F32),
        ] + [pltpu.VMEM((tt + 2, b, fb), F32)] * 6,
        input_output_aliases={9: 0},
        compiler_params=_params("arbitrary", "arbitrary"),
        name="ffn_down",
    )(z, z, z, z, z, z, w_conv9, w_conv9, w_down, xs, mod, g_post)


def _pick(limit, total):
    t = min(limit, total)
    while total % t:
        t -= 1
    return t


def kernel(x, c, ctx, c_ctx, w_ada, b_ada, w_in, w_pool, pool_scale, ssm_a_re, ssm_a_im, ssm_log_dt,
           ssm_b_re, ssm_b_im, ssm_c_re, ssm_c_im, ssm_d, w_glu, w_out, g_pre_mix, g_post_mix,
           g_pre_ffn, g_post_ffn, w_up, w_conv, w_down):
    bsz, seq, d = x.shape
    n_ctx = ctx.shape[1]
    depth = w_ada.shape[0]
    mixw = w_in.shape[-1]
    sw = ssm_d.shape[-1]
    f = w_down.shape[1]
    assert bsz == SUBLANES and seq % GRID_W == 0 and n_ctx % GRID_W == 0
    assert sw % LANES == 0 and (mixw - sw) % sw == 0 and LANES % ssm_b_re.shape[-1] == 0
    s_all = seq + n_ctx
    tt = GRID_W
    n_lat_tiles = seq // tt
    n_ctx_tiles = n_ctx // tt
    n_tiles = n_lat_tiles + n_ctx_tiles
    tt_up = _pick(2 * GRID_W, n_ctx)
    assert seq % tt_up == 0
    nb_up = _pick(1024, 2 * f)
    fb = _pick(512, f)
    lat_chunks = _pick(64, seq // SSM_CHUNK)
    ctx_chunks = _pick(64, n_ctx // SSM_CHUNK)
    assert (seq // SSM_CHUNK) % ctx_chunks == 0

    xs = jnp.concatenate([x, ctx], axis=1).transpose(1, 0, 2)
    c16 = jnp.concatenate([c, jnp.broadcast_to(c_ctx[None, :], (SUBLANES, d))], axis=0)
    mod_all = _ada_all(c16, w_ada, b_ada, _pick(1024, w_ada.shape[-1]))

    row = lambda v: v.reshape(1, -1).astype(F32)
    for l in range(depth):
        last = l == depth - 1
        mod = mod_all[l]
        mats = [_ssm_matrices(ssm_a_re[l, dr], ssm_a_im[l, dr], ssm_log_dt[l, dr], ssm_b_re[l, dr],
                              ssm_b_im[l, dr], ssm_c_re[l, dr], ssm_c_im[l, dr], dr == 1)
                for dr in range(2)]
        mats = tuple(jnp.stack(m) for m in zip(*mats))

        u = _premix(xs, mod, row(g_pre_mix[l]), w_in[l].astype(BF16), tt, n_lat_tiles)
        u5 = u.reshape(s_all // SSM_CHUNK, SSM_CHUNK, bsz, mixw)
        nbk = sw // LANES
        sb = mats[0].shape[-1]
        h0 = jnp.zeros((2, nbk, bsz, sb), F32)
        lane0 = (mixw - sw) // LANES
        y5, h_ctx = _ssm(u5, mats, h0, None, ctx_chunks, (seq // SSM_CHUNK) // ctx_chunks,
                         (n_ctx // SSM_CHUNK) // ctx_chunks, lane0)
        y5, _ = _ssm(u5, mats, h_ctx, y5, lat_chunks, 0, (seq // SSM_CHUNK) // lat_chunks, lane0)
        y = y5.reshape(2, s_all, bsz, sw)

        nt = n_lat_tiles if last else n_tiles
        xs = _mix(u, y, xs, mod, w_pool[l].astype(BF16), row(pool_scale[l]), row(ssm_d[l]),
                  w_glu[l].astype(BF16), w_out[l].astype(BF16), row(g_post_mix[l]), tt, nt,
                  n_lat_tiles, seq, n_ctx)
        nt_up = (seq if last else s_all) // tt_up
        z = _ffn_up(xs, mod, row(g_pre_ffn[l]), w_up[l].astype(BF16), tt_up, nb_up, nt_up,
                    seq // tt_up)
        xs = _ffn_down(z, w_conv[l].reshape(9, 2 * f).astype(F32), w_down[l].astype(BF16), xs, mod,
                       row(g_post_ffn[l]), fb, nt, n_lat_tiles, n_ctx_tiles)
    return xs[:seq].transpose(1, 0, 2)
```

```python
import functools

import jax
import jax.numpy as jnp
from jax import lax
from jax.experimental import pallas as pl
from jax.experimental.pallas import tpu as pltpu

GRID_W = 64
POOL_WINDOWS = (2, 4, 8, 16)
POOL_HALO = max(POOL_WINDOWS) // 2
EPS = 1e-6
SUBLANES = 8
LANES = 128
SSM_CHUNK = 8
VMEM_LIMIT_BYTES = 60000 * 1024

F32 = jnp.float32
BF16 = jnp.bfloat16


def _params(*sem):
    return pltpu.CompilerParams(dimension_semantics=sem, vmem_limit_bytes=VMEM_LIMIT_BYTES)


def _const_spec(shape):
    zeros = (0,) * len(shape)
    return pl.BlockSpec(shape, lambda *_: zeros, pipeline_mode=pl.Buffered(1))


def _rms(v, gain):
    ms = jnp.mean(v * v, axis=-1, keepdims=True)
    return v * lax.rsqrt(ms + EPS) * gain


def _ada_kernel(c_ref, w_ref, b_ref, o_ref):
    s = jax.nn.silu(c_ref[...]).astype(BF16)
    r = jnp.dot(s, w_ref[...].astype(BF16), preferred_element_type=F32) + b_ref[...]
    o_ref[...] = r.reshape(o_ref.shape)


def _ada_all(c16, w_ada, b_ada, nb):
    depth, d, n = w_ada.shape
    return pl.pallas_call(
        _ada_kernel,
        grid=(depth, n // nb),
        in_specs=[
            pl.BlockSpec((2 * SUBLANES, d), lambda l, j: (0, 0)),
            pl.BlockSpec((None, d, nb), lambda l, j: (l, 0, j)),
            pl.BlockSpec((None, 1, nb), lambda l, j: (l, 0, j)),
        ],
        out_specs=pl.BlockSpec((None, 2, SUBLANES, nb), lambda l, j: (l, 0, 0, j)),
        out_shape=jax.ShapeDtypeStruct((depth, 2, SUBLANES, n), F32),
        compiler_params=_params("arbitrary", "arbitrary"),
        name="ada",
    )(c16, w_ada, b_ada.reshape(depth, 1, n))


def _premix_kernel(x_ref, sh_ref, sc_ref, g_ref, w_ref, u_ref):
    tt, b, d = x_ref.shape
    h = _rms(x_ref[...], g_ref[...]) * (1.0 + sc_ref[...]) + sh_ref[...]
    u = jnp.dot(h.reshape(tt * b, d).astype(BF16), w_ref[...], preferred_element_type=F32)
    u_ref[...] = u.reshape(u_ref.shape)


def _premix(xs, mod, g, w, tt, n_lat_tiles):
    s, b, d = xs.shape
    n = w.shape[1]
    sel = lambda i: jnp.where(i >= n_lat_tiles, 1, 0)
    return pl.pallas_call(
        _premix_kernel,
        grid=(s // tt,),
        in_specs=[
            pl.BlockSpec((tt, b, d), lambda i: (i, 0, 0)),
            pl.BlockSpec((None, b, d), lambda i: (sel(i), 0, 0)),
            pl.BlockSpec((None, b, d), lambda i: (sel(i), 0, 1)),
            _const_spec((1, d)),
            _const_spec((d, n)),
        ],
        out_specs=pl.BlockSpec((tt, b, n), lambda i: (i, 0, 0)),
        out_shape=jax.ShapeDtypeStruct((s, b, n), F32),
        compiler_params=_params("arbitrary"),
        name="premix",
    )(xs, mod, mod, g, w)


def _ssm_kernel(u_ref, mo_ref, my_ref, lam_ref, h0_ref, y_ref, hf_ref, s_ref, hin_ref, hc_ref, *,
                chunks):
    c_n = chunks
    sb = hc_ref.shape[-1]
    half = sb // 2
    d = pl.program_id(0)
    t = pl.program_id(2)

    @pl.when(t == 0)
    def _():
        hc_ref[...] = h0_ref[...]

    u_rows = jnp.concatenate(
        [u_ref[:, k].reshape(c_n * SUBLANES, LANES).astype(BF16) for k in range(SSM_CHUNK)], axis=-1)
    s_ref[...] = jnp.dot(u_rows, mo_ref[...], preferred_element_type=F32).reshape(s_ref.shape)

    lre = jnp.broadcast_to(lam_ref[0:1, :], (SUBLANES, half))
    lim = jnp.broadcast_to(lam_ref[1:2, :], (SUBLANES, half))

    def step(i, carry):
        hr, hi = carry
        c = jnp.where(d == 0, i, c_n - 1 - i)
        hin_ref[c, :, :half] = hr
        hin_ref[c, :, half:] = hi
        sr = s_ref[c, :, :half]
        si = s_ref[c, :, half:]
        return lre * hr - lim * hi + sr, lre * hi + lim * hr + si

    hr, hi = lax.fori_loop(0, c_n, step, (hc_ref[:, :half], hc_ref[:, half:]))
    hc_ref[:, :half] = hr
    hc_ref[:, half:] = hi
    hf_ref[:, :half] = hr
    hf_ref[:, half:] = hi

    h_rows = hin_ref[...].reshape(c_n * SUBLANES, sb).astype(BF16)
    y = jnp.dot(jnp.concatenate([u_rows, h_rows], axis=-1), my_ref[...], preferred_element_type=F32)
    for k in range(SSM_CHUNK):
        y_ref[:, k] = y[:, k * LANES:(k + 1) * LANES].reshape(c_n, SUBLANES, LANES)


def _ssm(u5, mats, h0, chunks, block_off, n_tiles, ssm_lane_block0):
    m_out, m_y, lam8 = mats
    b = u5.shape[2]
    _, nb, kk, sb = m_out.shape
    tile = lambda d, t: jnp.where(d == 0, t, n_tiles - 1 - t)
    y_shape = (2, n_tiles * chunks, SSM_CHUNK, b, nb * LANES)
    return pl.pallas_call(
        functools.partial(_ssm_kernel, chunks=chunks),
        grid=(2, nb, n_tiles),
        in_specs=[
            pl.BlockSpec((chunks, SSM_CHUNK, b, LANES),
                         lambda d, j, t: (tile(d, t) + block_off, 0, 0, ssm_lane_block0 + j)),
            pl.BlockSpec((None, None, kk, sb), lambda d, j, t: (d, j, 0, 0)),
            pl.BlockSpec((None, None, kk + sb, kk), lambda d, j, t: (d, j, 0, 0)),
            pl.BlockSpec((None, None, 2, sb // 2), lambda d, j, t: (d, j, 0, 0)),
            pl.BlockSpec((None, None, b, sb), lambda d, j, t: (d, j, 0, 0)),
        ],
        out_specs=[
            pl.BlockSpec((None, chunks, SSM_CHUNK, b, LANES), lambda d, j, t: (d, tile(d, t), 0, 0, j)),
            pl.BlockSpec((None, None, b, sb), lambda d, j, t: (d, j, 0, 0)),
        ],
        out_shape=[jax.ShapeDtypeStruct(y_shape, F32), jax.ShapeDtypeStruct(h0.shape, F32)],
        scratch_shapes=[
            pltpu.VMEM((chunks, b, sb), F32),
            pltpu.VMEM((chunks, b, sb), F32),
            pltpu.VMEM((b, sb), F32),
        ],
        compiler_params=_params("arbitrary", "arbitrary", "arbitrary"),
        name="ssm",
    )(u5, m_out, m_y, lam8, h0)


def _ssm_matrices(a_re, a_im, log_dt, b_re, b_im, c_re, c_im, reverse):
    g, p = a_re.shape
    h = b_re.shape[-1]
    t_n = SSM_CHUNK
    gpb = LANES // h
    nb = g // gpb
    a_re = a_re.astype(F32)
    a_im = a_im.astype(F32)
    dt = jnp.exp(log_dt.astype(F32))[:, None]
    mag = jnp.exp(a_re * dt)
    lam_re = mag * jnp.cos(a_im * dt)
    lam_im = mag * jnp.sin(a_im * dt)
    denom = a_re * a_re + a_im * a_im
    nr, ni = lam_re - 1.0, lam_im
    f_re = ((nr * a_re + ni * a_im) / denom)[..., None]
    f_im = ((ni * a_re - nr * a_im) / denom)[..., None]
    b_re = b_re.astype(F32)
    b_im = b_im.astype(F32)
    bb_re = f_re * b_re - f_im * b_im
    bb_im = f_re * b_im + f_im * b_re
    c_re = c_re.astype(F32)
    c_im = c_im.astype(F32)
    pr, pi = [jnp.ones_like(lam_re)], [jnp.zeros_like(lam_im)]
    for _ in range(t_n):
        pr, pi = (pr + [pr[-1] * lam_re - pi[-1] * lam_im], pi + [pr[-1] * lam_im + pi[-1] * lam_re])
    pr = jnp.stack(pr)
    pi = jnp.stack(pi)

    q_re = pr[:t_n, :, :, None] * bb_re - pi[:t_n, :, :, None] * bb_im
    q_im = pr[:t_n, :, :, None] * bb_im + pi[:t_n, :, :, None] * bb_re
    hi = lax.Precision.HIGHEST
    taps = (jnp.einsum('gop,kgpi->kgoi', c_re, q_re, precision=hi)
            - jnp.einsum('gop,kgpi->kgoi', c_im, q_im, precision=hi))
    lag = jnp.arange(t_n)[None, :] - jnp.arange(t_n)[:, None]
    toep = jnp.where((lag >= 0)[:, :, None, None, None], taps[jnp.clip(lag, 0, t_n - 1)], 0.0)
    toep = toep.reshape(t_n, t_n, nb, gpb, h, h)
    cr, ci = c_re[None], c_im[None]
    lr, li = pr[1:, :, None, :], pi[1:, :, None, :]
    w_in = jnp.stack([cr * lr - ci * li, -(cr * li + ci * lr)])
    w_in = w_in.reshape(2, t_n, nb, gpb, h, p)
    w_out = jnp.stack([q_re[::-1], q_im[::-1]]).reshape(2, t_n, nb, gpb, p, h)
    if reverse:
        toep = toep[::-1, ::-1]
        w_in = w_in[:, ::-1]
        w_out = w_out[:, ::-1]
    k_dim = t_n * gpb * h
    sb = 2 * gpb * p

    def on_group_diagonal(compact, row_width, col_width, n_cols):
        n_rows, n_compact = compact.shape[1:]
        col = jnp.arange(n_cols)
        src = (col // (gpb * col_width)) * col_width + col % col_width
        place = (jnp.arange(n_compact)[:, None] == src[None, :]).astype(F32)
        full = jnp.einsum('jrk,kc->jrc', compact, place)
        row_group = (jnp.arange(n_rows) // row_width) % gpb
        col_group = (col // col_width) % gpb
        return jnp.where(row_group[:, None] == col_group[None, :], full, 0.0).astype(BF16)

    m_toep = on_group_diagonal(toep.transpose(2, 0, 3, 5, 1, 4).reshape(nb, k_dim, t_n * h), h, h, k_dim)
    m_in = on_group_diagonal(w_in.transpose(2, 0, 3, 5, 1, 4).reshape(nb, sb, t_n * h), p, h, k_dim)
    m_out = on_group_diagonal(w_out.transpose(2, 1, 3, 5, 0, 4).reshape(nb, k_dim, 2 * p), h, p, sb)
    m_y = jnp.concatenate([m_toep, m_in], axis=1)
    lam8 = jnp.stack([pr[t_n].reshape(nb, gpb * p), pi[t_n].reshape(nb, gpb * p)], axis=1)
    return m_out, m_y, lam8


def _mix_kernel(um_ref, up_ref, un_ref, us_ref, yl_ref, yc_ref, x_ref, gate_ref, wp_ref, ps_ref,
                sd_ref, wg_ref, wo_ref, gp_ref, o_ref, ext_ref, cat_ref, *, n_lat_tiles, seq, ctx):
    tt, b, pw_all = um_ref.shape
    pg = pw_all // len(POOL_WINDOWS)
    sw = us_ref.shape[-1]
    rows = tt * b
    i = pl.program_id(0)
    is_ctx = i >= n_lat_tiles
    t0 = jnp.where(is_ctx, i - n_lat_tiles, i) * tt
    seg = jnp.where(is_ctx, ctx, seq)
    has_prev = (t0 > 0).astype(F32)
    has_next = (t0 + tt < seg).astype(F32)
    hl = POOL_HALO
    ext_ref[0:hl] = up_ref[...] * has_prev
    ext_ref[hl:hl + tt] = um_ref[...]
    ext_ref[hl + tt:hl + tt + hl] = un_ref[...] * has_next

    tl = t0 + lax.broadcasted_iota(jnp.int32, (tt, b, pg), 0)
    for g, w in enumerate(POOL_WINDOWS):
        sl = slice(g * pg, (g + 1) * pg)
        lo = hl - w // 2
        acc = ext_ref[lo:lo + tt, :, sl]
        for k in range(1, w):
            acc = acc + ext_ref[lo + k:lo + k + tt, :, sl]
        cnt = (jnp.minimum(tl + w // 2, seg) - jnp.maximum(tl - w // 2, 0)).astype(F32)
        p = acc / cnt - ext_ref[hl:hl + tt, :, sl]
        po = jnp.dot(p.reshape(rows, pg).astype(BF16), wp_ref[g], preferred_element_type=F32)
        cat_ref[:, sl] = (po * ps_ref[:, sl]).astype(BF16)

    y = jnp.where(is_ctx, yc_ref[0] + yc_ref[1], yl_ref[0] + yl_ref[1])
    yv = y.reshape(rows, sw) + sd_ref[...] * us_ref[...].reshape(rows, sw)
    yv = jax.nn.gelu(yv)
    glu = jax.nn.sigmoid(jnp.dot(yv.astype(BF16), wg_ref[...], preferred_element_type=F32))
    cat_ref[:, pw_all:] = (yv * glu).astype(BF16)

    mix = jnp.dot(cat_ref[...], wo_ref[...], preferred_element_type=F32)
    o_ref[...] = x_ref[...] + gate_ref[...] * _rms(mix, gp_ref[...]).reshape(o_ref.shape)


def _mix(u, y_lat, y_ctx, xs, mod, w_pool, pool_scale, ssm_d, w_glu, w_out, g_post, tt, n_tiles,
         n_lat_tiles, seq, ctx):
    s, b, d = xs.shape
    mixw = u.shape[-1]
    sw = y_lat.shape[-1]
    pw_all = mixw - sw
    hl = POOL_HALO
    hpt = tt // hl
    last_halo = n_tiles * hpt - 1
    sel = lambda i: jnp.where(i >= n_lat_tiles, 1, 0)
    return pl.pallas_call(
        functools.partial(_mix_kernel, n_lat_tiles=n_lat_tiles, seq=seq, ctx=ctx),
        grid=(n_tiles,),
        in_specs=[
            pl.BlockSpec((tt, b, pw_all), lambda i: (i, 0, 0)),
            pl.BlockSpec((hl, b, pw_all), lambda i: (jnp.maximum(i * hpt - 1, 0), 0, 0)),
            pl.BlockSpec((hl, b, pw_all), lambda i: (jnp.minimum((i + 1) * hpt, last_halo), 0, 0)),
            pl.BlockSpec((tt, b, sw), lambda i: (i, 0, pw_all // sw)),
            pl.BlockSpec((2, tt, b, sw), lambda i: (0, jnp.minimum(i, n_lat_tiles - 1), 0, 0)),
            pl.BlockSpec((2, tt, b, sw), lambda i: (0, jnp.maximum(i - n_lat_tiles, 0), 0, 0)),
            pl.BlockSpec((tt, b, d), lambda i: (i, 0, 0)),
            pl.BlockSpec((None, b, d), lambda i: (sel(i), 0, 2)),
            _const_spec(w_pool.shape),
            _const_spec((1, pw_all)),
            _const_spec((1, sw)),
            _const_spec(w_glu.shape),
            _const_spec(w_out.shape),
            _const_spec((1, d)),
        ],
        out_specs=pl.BlockSpec((tt, b, d), lambda i: (i, 0, 0)),
        out_shape=jax.ShapeDtypeStruct(xs.shape, F32),
        scratch_shapes=[
            pltpu.VMEM((tt + 2 * hl, b, pw_all), F32),
            pltpu.VMEM((tt * b, mixw), BF16),
        ],
        input_output_aliases={6: 0},
        compiler_params=_params("arbitrary"),
        name="mix",
    )(u, u, u, u, y_lat, y_ctx, xs, mod, w_pool, pool_scale, ssm_d, w_glu, w_out, g_post)


def _ffn_up_kernel(x_ref, sh_ref, sc_ref, g_ref, w_ref, z_ref, h_ref):
    @pl.when(pl.program_id(1) == 0)
    def _():
        h = _rms(x_ref[...], g_ref[...]) * (1.0 + sc_ref[...]) + sh_ref[...]
        h_ref[...] = h.reshape(h_ref.shape).astype(BF16)

    z = jnp.dot(h_ref[...], w_ref[...], preferred_element_type=F32).astype(BF16)
    fb = z_ref.shape[-1]
    for k in range(z_ref.shape[0]):
        z_ref[k] = z[:, k * fb:(k + 1) * fb]


def _ffn_up(xs, mod, g, w_up, tt, nb, fb, n_tiles, n_lat_tiles):
    s, b, d = xs.shape
    n = w_up.shape[1]
    sel = lambda i: jnp.where(i >= n_lat_tiles, 1, 0)
    return pl.pallas_call(
        _ffn_up_kernel,
        grid=(n_tiles, n // nb),
        in_specs=[
            pl.BlockSpec((tt, b, d), lambda i, j: (i, 0, 0)),
            pl.BlockSpec((None, b, d), lambda i, j: (sel(i), 0, 3)),
            pl.BlockSpec((None, b, d), lambda i, j: (sel(i), 0, 4)),
            _const_spec((1, d)),
            pl.BlockSpec((d, nb), lambda i, j: (0, j)),
        ],
        out_specs=pl.BlockSpec((nb // fb, tt * b, fb), lambda i, j: (j, i, 0)),
        out_shape=jax.ShapeDtypeStruct((n // fb, n_tiles * tt * b, fb), BF16),
        scratch_shapes=[pltpu.VMEM((tt * b, d), BF16)],
        compiler_params=_params("arbitrary", "arbitrary"),
        name="ffn_up",
    )(xs, mod, mod, g, w_up)


FFN_QUARTERS = 4


def _ffn_down_kernel(va_ref, vm_ref, vb_ref, ga_ref, gm_ref, gb_ref, kv_ref, kg_ref, wd_ref, x_ref,
                     gate_ref, gp_ref, o_ref, acc_ref, act_ref, kb_ref, eva_ref, evm_ref, evb_ref,
                     ega_ref, egm_ref, egb_ref, *, n_lat_tiles, n_ctx_tiles):
    w, b, fb = evm_ref.shape[0] - 2, evm_ref.shape[1], evm_ref.shape[2]
    i = pl.program_id(0)
    j = pl.program_id(1)
    nf = pl.num_programs(1) - 1
    is_ctx = i >= n_lat_tiles
    is_lat = jnp.logical_not(is_ctx)
    ic = i - n_lat_tiles
    vert_up = jnp.logical_and(is_lat, i > 0).astype(F32)
    vert_dn = jnp.logical_and(is_lat, i < n_lat_tiles - 1).astype(F32)
    hz_prev = jnp.logical_and(is_ctx, ic > 0).astype(F32)
    hz_next = jnp.logical_and(is_ctx, ic < n_ctx_tiles - 1).astype(F32)
    slot = j % 2

    @pl.when(j == 0)
    def _():
        acc_ref[...] = jnp.zeros(acc_ref.shape, F32)
        act_ref[1] = jnp.zeros(act_ref.shape[1:], BF16)

    streams = ((va_ref, vm_ref, vb_ref, eva_ref, evm_ref, evb_ref, kv_ref),
               (ga_ref, gm_ref, gb_ref, ega_ref, egm_ref, egb_ref, kg_ref))
    pair = 2 * b
    zero_row = jnp.zeros((1, b, fb), F32)
    for which, (a_ref, m_ref, b_ref, ea_ref, em_ref, eb_ref, k_ref) in enumerate(streams):
        ea_ref[0:1] = zero_row
        ea_ref[w + 1:w + 2] = zero_row
        eb_ref[0:1] = zero_row
        eb_ref[w + 1:w + 2] = zero_row
        last_above = a_ref[(w - 2) * b:w * b, :].astype(F32)[b:pair]
        first_below = b_ref[0:pair, :].astype(F32)[0:b]
        em_ref[0:1] = (last_above * hz_prev).reshape(1, b, fb)
        em_ref[w + 1:w + 2] = (first_below * hz_next).reshape(1, b, fb)
        for r, flag in enumerate((vert_up, None, vert_dn)):
            for dc in range(3):
                kr = k_ref[3 * r + dc:3 * r + dc + 1, :]
                kr = kr if flag is None else kr * flag
                kb_ref[9 * which + 3 * r + dc] = jnp.broadcast_to(kr, (b, fb))

    tq = w // FFN_QUARTERS
    nq_lanes = acc_ref.shape[-1] // FFN_QUARTERS
    tp = SUBLANES

    def stage(q):
        for a_ref, m_ref, b_ref, ea_ref, em_ref, eb_ref, _ in streams:
            for src, dst in ((a_ref, ea_ref), (m_ref, em_ref), (b_ref, eb_ref)):
                blk = src[q * tq * b:(q + 1) * tq * b, :].astype(F32)
                dst[1 + q * tq:1 + (q + 1) * tq] = blk.reshape(tq, b, fb)

    def conv_piece(t0, lanes, which):
        out = None
        for r, e_ref in enumerate(streams[which][3:6]):
            for dc in range(3):
                term = e_ref[t0 + dc:t0 + dc + tp, :, lanes] * kb_ref[9 * which + 3 * r + dc, :, lanes]
                out = term if out is None else out + term
        return out

    stage(0)
    for q in range(FFN_QUARTERS):
        if q + 1 < FFN_QUARTERS:
            stage(q + 1)
        cols = slice(q * nq_lanes, (q + 1) * nq_lanes)
        acc_ref[:, cols] += jnp.dot(act_ref[1 - slot], wd_ref[:, cols], preferred_element_type=F32)
        for lc in range(fb // LANES):
            lanes = slice(lc * LANES, (lc + 1) * LANES)
            for t0 in range(q * tq, (q + 1) * tq, tp):
                val = conv_piece(t0, lanes, 0)
                gat = conv_piece(t0, lanes, 1)
                act = (val * jax.nn.silu(gat)).reshape(tp * b, LANES).astype(BF16)
                act_ref[slot, t0 * b:(t0 + tp) * b, lanes] = act

    @pl.when(j == nf)
    def _():
        o_ref[...] = x_ref[...] + gate_ref[...] * _rms(acc_ref[...], gp_ref[...]).reshape(o_ref.shape)


def _ffn_down(z, w_conv9, w_down, xs, mod, g_post, fb, n_tiles, n_lat_tiles, n_ctx_tiles):
    s, b, d = xs.shape
    f = w_down.shape[0]
    nf = f // fb
    tt = GRID_W
    rows = tt * b
    up = lambda i: jnp.maximum(i - 1, 0)
    dn = lambda i: jnp.minimum(i + 1, n_tiles - 1)
    sel = lambda i: jnp.where(i >= n_lat_tiles, 1, 0)
    cur = lambda j: jnp.minimum(j, nf - 1)
    prv = lambda j: jnp.maximum(j - 1, 0)
    zspec = lambda row, col0: pl.BlockSpec((None, rows, fb), lambda i, j: (col0 + cur(j), row(i), 0))
    same = lambda i: i
    assert tt % (FFN_QUARTERS * SUBLANES) == 0 and d % (FFN_QUARTERS * LANES) == 0
    return pl.pallas_call(
        functools.partial(_ffn_down_kernel, n_lat_tiles=n_lat_tiles, n_ctx_tiles=n_ctx_tiles),
        grid=(n_tiles, nf + 1),
        in_specs=[
            zspec(up, 0), zspec(same, 0), zspec(dn, 0),
            zspec(up, nf), zspec(same, nf), zspec(dn, nf),
            pl.BlockSpec((9, fb), lambda i, j: (0, cur(j))),
            pl.BlockSpec((9, fb), lambda i, j: (0, nf + cur(j))),
            pl.BlockSpec((fb, d), lambda i, j: (prv(j), 0)),
            pl.BlockSpec((tt, b, d), lambda i, j: (i, 0, 0)),
            pl.BlockSpec((None, b, d), lambda i, j: (sel(i), 0, 5)),
            _const_spec((1, d)),
        ],
        out_specs=pl.BlockSpec((tt, b, d), lambda i, j: (i, 0, 0)),
        out_shape=jax.ShapeDtypeStruct(xs.shape, F32),
        scratch_shapes=[
            pltpu.VMEM((rows, d), F32),
            pltpu.VMEM((2, rows, fb), BF16),
            pltpu.VMEM((18, b, fb), F32),
        ] + [pltpu.VMEM((tt + 2, b, fb), F32)] * 6,
        input_output_aliases={9: 0},
        compiler_params=_params("arbitrary", "arbitrary"),
        name="ffn_down",
    )(z, z, z, z, z, z, w_conv9, w_conv9, w_down, xs, mod, g_post)


def _pick(limit, total):
    t = min(limit, total)
    while total % t:
        t -= 1
    return t


def kernel(x, c, ctx, c_ctx, w_ada, b_ada, w_in, w_pool, pool_scale, ssm_a_re, ssm_a_im, ssm_log_dt,
           ssm_b_re, ssm_b_im, ssm_c_re, ssm_c_im, ssm_d, w_glu, w_out, g_pre_mix, g_post_mix,
           g_pre_ffn, g_post_ffn, w_up, w_conv, w_down):
    bsz, seq, d = x.shape
    n_ctx = ctx.shape[1]
    depth = w_ada.shape[0]
    mixw = w_in.shape[-1]
    sw = ssm_d.shape[-1]
    f = w_down.shape[1]
    assert bsz == SUBLANES and seq % GRID_W == 0 and n_ctx % GRID_W == 0
    assert sw % LANES == 0 and (mixw - sw) % sw == 0 and LANES % ssm_b_re.shape[-1] == 0
    s_all = seq + n_ctx
    tt = GRID_W
    n_lat_tiles = seq // tt
    n_ctx_tiles = n_ctx // tt
    n_tiles = n_lat_tiles + n_ctx_tiles
    tt_up = _pick(2 * GRID_W, n_ctx)
    assert seq % tt_up == 0
    nb_up = _pick(1024, 2 * f)
    fb = _pick(512, f)
    lat_chunks = _pick(64, seq // SSM_CHUNK)
    ctx_chunks = _pick(64, n_ctx // SSM_CHUNK)
    assert (seq // SSM_CHUNK) % ctx_chunks == 0

    xs = jnp.concatenate([x, ctx], axis=1).transpose(1, 0, 2)
    c16 = jnp.concatenate([c, jnp.broadcast_to(c_ctx[None, :], (SUBLANES, d))], axis=0)
    mod_all = _ada_all(c16, w_ada, b_ada, _pick(1024, w_ada.shape[-1]))

    row = lambda v: v.reshape(1, -1).astype(F32)
    for l in range(depth):
        last = l == depth - 1
        mod = mod_all[l]
        mats = [_ssm_matrices(ssm_a_re[l, dr], ssm_a_im[l, dr], ssm_log_dt[l, dr], ssm_b_re[l, dr],
                              ssm_b_im[l, dr], ssm_c_re[l, dr], ssm_c_im[l, dr], dr == 1)
                for dr in range(2)]
        mats = tuple(jnp.stack(m) for m in zip(*mats))

        u = _premix(xs, mod, row(g_pre_mix[l]), w_in[l].astype(BF16), tt, n_lat_tiles)
        u5 = u.reshape(s_all // SSM_CHUNK, SSM_CHUNK, bsz, mixw)
        nbk = sw // LANES
        sb = mats[0].shape[-1]
        h0 = jnp.zeros((2, nbk, bsz, sb), F32)
        lane0 = (mixw - sw) // LANES
        y_ctx, h_ctx = _ssm(u5, mats, h0, ctx_chunks, (seq // SSM_CHUNK) // ctx_chunks,
                            (n_ctx // SSM_CHUNK) // ctx_chunks, lane0)
        y_lat, _ = _ssm(u5, mats, h_ctx, lat_chunks, 0, (seq // SSM_CHUNK) // lat_chunks, lane0)

        nt = n_lat_tiles if last else n_tiles
        xs = _mix(u, y_lat.reshape(2, seq, bsz, sw), y_ctx.reshape(2, n_ctx, bsz, sw), xs, mod,
                  w_pool[l].astype(BF16), row(pool_scale[l]), row(ssm_d[l]), w_glu[l].astype(BF16),
                  w_out[l].astype(BF16), row(g_post_mix[l]), tt, nt, n_lat_tiles, seq, n_ctx)
        nt_up = (seq if last else s_all) // tt_up
        z = _ffn_up(xs, mod, row(g_pre_ffn[l]), w_up[l].astype(BF16), tt_up, nb_up, fb, nt_up,
                    seq // tt_up)
        xs = _ffn_down(z, w_conv[l].reshape(9, 2 * f).astype(F32), w_down[l].astype(BF16), xs, mod,
                       row(g_post_ffn[l]), fb, nt, n_lat_tiles, n_ctx_tiles)
    return xs[:seq].transpose(1, 0, 2)
```

```python
import functools

import jax
import jax.numpy as jnp
from jax import lax
from jax.experimental import pallas as pl
from jax.experimental.pallas import tpu as pltpu

GRID_W = 64
POOL_WINDOWS = (2, 4, 8, 16)
POOL_HALO = max(POOL_WINDOWS) // 2
EPS = 1e-6
SUBLANES = 8
LANES = 128
SSM_CHUNK = 8
VMEM_LIMIT_BYTES = 60000 * 1024

F32 = jnp.float32
BF16 = jnp.bfloat16


def _params(*sem):
    return pltpu.CompilerParams(dimension_semantics=sem, vmem_limit_bytes=VMEM_LIMIT_BYTES)


def _const_spec(shape):
    zeros = (0,) * len(shape)
    return pl.BlockSpec(shape, lambda *_: zeros, pipeline_mode=pl.Buffered(1))


def _rms(v, gain):
    ms = jnp.mean(v * v, axis=-1, keepdims=True)
    return v * lax.rsqrt(ms + EPS) * gain


def _ada_kernel(c_ref, w_ref, b_ref, o_ref):
    s = jax.nn.silu(c_ref[...]).astype(BF16)
    r = jnp.dot(s, w_ref[...].astype(BF16), preferred_element_type=F32) + b_ref[...]
    o_ref[...] = r.reshape(o_ref.shape)


def _ada_all(c16, w_ada, b_ada, nb):
    depth, d, n = w_ada.shape
    return pl.pallas_call(
        _ada_kernel,
        grid=(depth, n // nb),
        in_specs=[
            pl.BlockSpec((2 * SUBLANES, d), lambda l, j: (0, 0)),
            pl.BlockSpec((None, d, nb), lambda l, j: (l, 0, j)),
            pl.BlockSpec((None, 1, nb), lambda l, j: (l, 0, j)),
        ],
        out_specs=pl.BlockSpec((None, 2, SUBLANES, nb), lambda l, j: (l, 0, 0, j)),
        out_shape=jax.ShapeDtypeStruct((depth, 2, SUBLANES, n), F32),
        compiler_params=_params("arbitrary", "arbitrary"),
        name="ada",
    )(c16, w_ada, b_ada.reshape(depth, 1, n))


def _premix_kernel(x_ref, sh_ref, sc_ref, g_ref, w_ref, u_ref):
    tt, b, d = x_ref.shape
    h = _rms(x_ref[...], g_ref[...]) * (1.0 + sc_ref[...]) + sh_ref[...]
    u = jnp.dot(h.reshape(tt * b, d).astype(BF16), w_ref[...], preferred_element_type=F32)
    u_ref[...] = u.reshape(u_ref.shape)


def _premix(xs, mod, g, w, tt, n_lat_tiles):
    s, b, d = xs.shape
    n = w.shape[1]
    sel = lambda i: jnp.where(i >= n_lat_tiles, 1, 0)
    return pl.pallas_call(
        _premix_kernel,
        grid=(s // tt,),
        in_specs=[
            pl.BlockSpec((tt, b, d), lambda i: (i, 0, 0)),
            pl.BlockSpec((None, b, d), lambda i: (sel(i), 0, 0)),
            pl.BlockSpec((None, b, d), lambda i: (sel(i), 0, 1)),
            _const_spec((1, d)),
            _const_spec((d, n)),
        ],
        out_specs=pl.BlockSpec((tt, b, n), lambda i: (i, 0, 0)),
        out_shape=jax.ShapeDtypeStruct((s, b, n), F32),
        compiler_params=_params("arbitrary"),
        name="premix",
    )(xs, mod, mod, g, w)


def _ssm_kernel(u_ref, mo_ref, my_ref, lam_ref, h0_ref, y_ref, hf_ref, s_ref, hin_ref, hc_ref, *,
                chunks):
    c_n = chunks
    sb = hc_ref.shape[-1]
    half = sb // 2
    d = pl.program_id(0)
    t = pl.program_id(2)

    @pl.when(t == 0)
    def _():
        hc_ref[...] = h0_ref[...]

    u_rows = jnp.concatenate(
        [u_ref[:, k].reshape(c_n * SUBLANES, LANES).astype(BF16) for k in range(SSM_CHUNK)], axis=-1)
    s_ref[...] = jnp.dot(u_rows, mo_ref[...], preferred_element_type=F32).reshape(s_ref.shape)

    lre = jnp.broadcast_to(lam_ref[0:1, :], (SUBLANES, half))
    lim = jnp.broadcast_to(lam_ref[1:2, :], (SUBLANES, half))

    def step(i, carry):
        hr, hi = carry
        c = jnp.where(d == 0, i, c_n - 1 - i)
        hin_ref[c, :, :half] = hr
        hin_ref[c, :, half:] = hi
        sr = s_ref[c, :, :half]
        si = s_ref[c, :, half:]
        return lre * hr - lim * hi + sr, lre * hi + lim * hr + si

    hr, hi = lax.fori_loop(0, c_n, step, (hc_ref[:, :half], hc_ref[:, half:]))
    hc_ref[:, :half] = hr
    hc_ref[:, half:] = hi
    hf_ref[:, :half] = hr
    hf_ref[:, half:] = hi

    h_rows = hin_ref[...].reshape(c_n * SUBLANES, sb).astype(BF16)
    y = jnp.dot(jnp.concatenate([u_rows, h_rows], axis=-1), my_ref[...], preferred_element_type=F32)
    for k in range(SSM_CHUNK):
        y_ref[:, k] = y[:, k * LANES:(k + 1) * LANES].reshape(c_n, SUBLANES, LANES)


def _ssm(u5, mats, h0, chunks, block_off, n_tiles, ssm_lane_block0):
    m_out, m_y, lam8 = mats
    b = u5.shape[2]
    _, nb, kk, sb = m_out.shape
    tile = lambda d, t: jnp.where(d == 0, t, n_tiles - 1 - t)
    y_shape = (2, n_tiles * chunks, SSM_CHUNK, b, nb * LANES)
    return pl.pallas_call(
        functools.partial(_ssm_kernel, chunks=chunks),
        grid=(2, nb, n_tiles),
        in_specs=[
            pl.BlockSpec((chunks, SSM_CHUNK, b, LANES),
                         lambda d, j, t: (tile(d, t) + block_off, 0, 0, ssm_lane_block0 + j)),
            pl.BlockSpec((None, None, kk, sb), lambda d, j, t: (d, j, 0, 0)),
            pl.BlockSpec((None, None, kk + sb, kk), lambda d, j, t: (d, j, 0, 0)),
            pl.BlockSpec((None, None, 2, sb // 2), lambda d, j, t: (d, j, 0, 0)),
            pl.BlockSpec((None, None, b, sb), lambda d, j, t: (d, j, 0, 0)),
        ],
        out_specs=[
            pl.BlockSpec((None, chunks, SSM_CHUNK, b, LANES), lambda d, j, t: (d, tile(d, t), 0, 0, j)),
            pl.BlockSpec((None, None, b, sb), lambda d, j, t: (d, j, 0, 0)),
        ],
        out_shape=[jax.ShapeDtypeStruct(y_shape, F32), jax.ShapeDtypeStruct(h0.shape, F32)],
        scratch_shapes=[
            pltpu.VMEM((chunks, b, sb), F32),
            pltpu.VMEM((chunks, b, sb), F32),
            pltpu.VMEM((b, sb), F32),
        ],
        compiler_params=_params("arbitrary", "arbitrary", "arbitrary"),
        name="ssm",
    )(u5, m_out, m_y, lam8, h0)


def _ssm_matrices(a_re, a_im, log_dt, b_re, b_im, c_re, c_im, reverse):
    g, p = a_re.shape
    h = b_re.shape[-1]
    t_n = SSM_CHUNK
    gpb = LANES // h
    nb = g // gpb
    a_re = a_re.astype(F32)
    a_im = a_im.astype(F32)
    dt = jnp.exp(log_dt.astype(F32))[:, None]
    mag = jnp.exp(a_re * dt)
    lam_re = mag * jnp.cos(a_im * dt)
    lam_im = mag * jnp.sin(a_im * dt)
    denom = a_re * a_re + a_im * a_im
    nr, ni = lam_re - 1.0, lam_im
    f_re = ((nr * a_re + ni * a_im) / denom)[..., None]
    f_im = ((ni * a_re - nr * a_im) / denom)[..., None]
    b_re = b_re.astype(F32)
    b_im = b_im.astype(F32)
    bb_re = f_re * b_re - f_im * b_im
    bb_im = f_re * b_im + f_im * b_re
    c_re = c_re.astype(F32)
    c_im = c_im.astype(F32)
    pr, pi = [jnp.ones_like(lam_re)], [jnp.zeros_like(lam_im)]
    for _ in range(t_n):
        pr, pi = (pr + [pr[-1] * lam_re - pi[-1] * lam_im], pi + [pr[-1] * lam_im + pi[-1] * lam_re])
    pr = jnp.stack(pr)
    pi = jnp.stack(pi)

    q_re = pr[:t_n, :, :, None] * bb_re - pi[:t_n, :, :, None] * bb_im
    q_im = pr[:t_n, :, :, None] * bb_im + pi[:t_n, :, :, None] * bb_re
    hi = lax.Precision.HIGHEST
    taps = (jnp.einsum('gop,kgpi->kgoi', c_re, q_re, precision=hi)
            - jnp.einsum('gop,kgpi->kgoi', c_im, q_im, precision=hi))
    lag = jnp.arange(t_n)[None, :] - jnp.arange(t_n)[:, None]
    toep = jnp.where((lag >= 0)[:, :, None, None, None], taps[jnp.clip(lag, 0, t_n - 1)], 0.0)
    toep = toep.reshape(t_n, t_n, nb, gpb, h, h)
    cr, ci = c_re[None], c_im[None]
    lr, li = pr[1:, :, None, :], pi[1:, :, None, :]
    w_in = jnp.stack([cr * lr - ci * li, -(cr * li + ci * lr)])
    w_in = w_in.reshape(2, t_n, nb, gpb, h, p)
    w_out = jnp.stack([q_re[::-1], q_im[::-1]]).reshape(2, t_n, nb, gpb, p, h)
    if reverse:
        toep = toep[::-1, ::-1]
        w_in = w_in[:, ::-1]
        w_out = w_out[:, ::-1]
    k_dim = t_n * gpb * h
    sb = 2 * gpb * p

    def on_group_diagonal(compact, row_width, col_width, n_cols):
        n_rows, n_compact = compact.shape[1:]
        col = jnp.arange(n_cols)
        src = (col // (gpb * col_width)) * col_width + col % col_width
        place = (jnp.arange(n_compact)[:, None] == src[None, :]).astype(F32)
        full = jnp.einsum('jrk,kc->jrc', compact, place)
        row_group = (jnp.arange(n_rows) // row_width) % gpb
        col_group = (col // col_width) % gpb
        return jnp.where(row_group[:, None] == col_group[None, :], full, 0.0).astype(BF16)

    m_toep = on_group_diagonal(toep.transpose(2, 0, 3, 5, 1, 4).reshape(nb, k_dim, t_n * h), h, h, k_dim)
    m_in = on_group_diagonal(w_in.transpose(2, 0, 3, 5, 1, 4).reshape(nb, sb, t_n * h), p, h, k_dim)
    m_out = on_group_diagonal(w_out.transpose(2, 1, 3, 5, 0, 4).reshape(nb, k_dim, 2 * p), h, p, sb)
    m_y = jnp.concatenate([m_toep, m_in], axis=1)
    lam8 = jnp.stack([pr[t_n].reshape(nb, gpb * p), pi[t_n].reshape(nb, gpb * p)], axis=1)
    return m_out, m_y, lam8


def _mix_kernel(um_ref, up_ref, un_ref, us_ref, yl_ref, yc_ref, x_ref, gate_ref, wp_ref, ps_ref,
                sd_ref, wg_ref, wo_ref, gp_ref, o_ref, ext_ref, cat_ref, *, n_lat_tiles, seq, ctx):
    tt, b, pw_all = um_ref.shape
    pg = pw_all // len(POOL_WINDOWS)
    sw = us_ref.shape[-1]
    rows = tt * b
    i = pl.program_id(0)
    is_ctx = i >= n_lat_tiles
    t0 = jnp.where(is_ctx, i - n_lat_tiles, i) * tt
    seg = jnp.where(is_ctx, ctx, seq)
    has_prev = (t0 > 0).astype(F32)
    has_next = (t0 + tt < seg).astype(F32)
    hl = POOL_HALO
    ext_ref[0:hl] = up_ref[...] * has_prev
    ext_ref[hl:hl + tt] = um_ref[...]
    ext_ref[hl + tt:hl + tt + hl] = un_ref[...] * has_next

    tl = t0 + lax.broadcasted_iota(jnp.int32, (tt, b, pg), 0)
    for g, w in enumerate(POOL_WINDOWS):
        sl = slice(g * pg, (g + 1) * pg)
        lo = hl - w // 2
        acc = ext_ref[lo:lo + tt, :, sl]
        for k in range(1, w):
            acc = acc + ext_ref[lo + k:lo + k + tt, :, sl]
        cnt = (jnp.minimum(tl + w // 2, seg) - jnp.maximum(tl - w // 2, 0)).astype(F32)
        p = acc / cnt - ext_ref[hl:hl + tt, :, sl]
        po = jnp.dot(p.reshape(rows, pg).astype(BF16), wp_ref[g], preferred_element_type=F32)
        cat_ref[:, sl] = (po * ps_ref[:, sl]).astype(BF16)

    y = jnp.where(is_ctx, yc_ref[0] + yc_ref[1], yl_ref[0] + yl_ref[1])
    yv = y.reshape(rows, sw) + sd_ref[...] * us_ref[...].reshape(rows, sw)
    yv = jax.nn.gelu(yv)
    glu = jax.nn.sigmoid(jnp.dot(yv.astype(BF16), wg_ref[...], preferred_element_type=F32))
    cat_ref[:, pw_all:] = (yv * glu).astype(BF16)

    mix = jnp.dot(cat_ref[...], wo_ref[...], preferred_element_type=F32)
    o_ref[...] = x_ref[...] + gate_ref[...] * _rms(mix, gp_ref[...]).reshape(o_ref.shape)


def _mix(u, y_lat, y_ctx, xs, mod, w_pool, pool_scale, ssm_d, w_glu, w_out, g_post, tt, n_tiles,
         n_lat_tiles, seq, ctx):
    s, b, d = xs.shape
    mixw = u.shape[-1]
    sw = y_lat.shape[-1]
    pw_all = mixw - sw
    hl = POOL_HALO
    hpt = tt // hl
    last_halo = n_tiles * hpt - 1
    sel = lambda i: jnp.where(i >= n_lat_tiles, 1, 0)
    return pl.pallas_call(
        functools.partial(_mix_kernel, n_lat_tiles=n_lat_tiles, seq=seq, ctx=ctx),
        grid=(n_tiles,),
        in_specs=[
            pl.BlockSpec((tt, b, pw_all), lambda i: (i, 0, 0)),
            pl.BlockSpec((hl, b, pw_all), lambda i: (jnp.maximum(i * hpt - 1, 0), 0, 0)),
            pl.BlockSpec((hl, b, pw_all), lambda i: (jnp.minimum((i + 1) * hpt, last_halo), 0, 0)),
            pl.BlockSpec((tt, b, sw), lambda i: (i, 0, pw_all // sw)),
            pl.BlockSpec((2, tt, b, sw), lambda i: (0, jnp.minimum(i, n_lat_tiles - 1), 0, 0)),
            pl.BlockSpec((2, tt, b, sw), lambda i: (0, jnp.maximum(i - n_lat_tiles, 0), 0, 0)),
            pl.BlockSpec((tt, b, d), lambda i: (i, 0, 0)),
            pl.BlockSpec((None, b, d), lambda i: (sel(i), 0, 2)),
            _const_spec(w_pool.shape),
            _const_spec((1, pw_all)),
            _const_spec((1, sw)),
            _const_spec(w_glu.shape),
            _const_spec(w_out.shape),
            _const_spec((1, d)),
        ],
        out_specs=pl.BlockSpec((tt, b, d), lambda i: (i, 0, 0)),
        out_shape=jax.ShapeDtypeStruct(xs.shape, F32),
        scratch_shapes=[
            pltpu.VMEM((tt + 2 * hl, b, pw_all), F32),
            pltpu.VMEM((tt * b, mixw), BF16),
        ],
        input_output_aliases={6: 0},
        compiler_params=_params("arbitrary"),
        name="mix",
    )(u, u, u, u, y_lat, y_ctx, xs, mod, w_pool, pool_scale, ssm_d, w_glu, w_out, g_post)


def _ffn_up_kernel(x_ref, sh_ref, sc_ref, g_ref, w_ref, z_ref, h_ref):
    @pl.when(pl.program_id(1) == 0)
    def _():
        h = _rms(x_ref[...], g_ref[...]) * (1.0 + sc_ref[...]) + sh_ref[...]
        h_ref[...] = h.reshape(h_ref.shape).astype(BF16)

    z = jnp.dot(h_ref[...], w_ref[...], preferred_element_type=F32).astype(BF16)
    fb = z_ref.shape[-1]
    for k in range(z_ref.shape[0]):
        z_ref[k] = z[:, k * fb:(k + 1) * fb]


def _ffn_up(xs, mod, g, w_up, tt, nb, fb, n_tiles, n_lat_tiles):
    s, b, d = xs.shape
    n = w_up.shape[1]
    sel = lambda i: jnp.where(i >= n_lat_tiles, 1, 0)
    return pl.pallas_call(
        _ffn_up_kernel,
        grid=(n_tiles, n // nb),
        in_specs=[
            pl.BlockSpec((tt, b, d), lambda i, j: (i, 0, 0)),
            pl.BlockSpec((None, b, d), lambda i, j: (sel(i), 0, 3)),
            pl.BlockSpec((None, b, d), lambda i, j: (sel(i), 0, 4)),
            _const_spec((1, d)),
            pl.BlockSpec((d, nb), lambda i, j: (0, j)),
        ],
        out_specs=pl.BlockSpec((nb // fb, tt * b, fb), lambda i, j: (j, i, 0)),
        out_shape=jax.ShapeDtypeStruct((n // fb, n_tiles * tt * b, fb), BF16),
        scratch_shapes=[pltpu.VMEM((tt * b, d), BF16)],
        compiler_params=_params("arbitrary", "arbitrary"),
        name="ffn_up",
    )(xs, mod, mod, g, w_up)


FFN_QUARTERS = 4


def _ffn_down_kernel(va_ref, vm_ref, vb_ref, ga_ref, gm_ref, gb_ref, kv_ref, kg_ref, wd_ref, x_ref,
                     gate_ref, gp_ref, o_ref, acc_ref, act_ref, kb_ref, eva_ref, evm0_ref, evm1_ref,
                     evb_ref, ega_ref, egm0_ref, egm1_ref, egb_ref, *, n_lat_tiles, n_ctx_tiles):
    w, b, fb = evm0_ref.shape[0] - 2, evm0_ref.shape[1], evm0_ref.shape[2]
    i = pl.program_id(0)
    j = pl.program_id(1)
    nf = pl.num_programs(1) - 1
    is_ctx = i >= n_lat_tiles
    is_lat = jnp.logical_not(is_ctx)
    ic = i - n_lat_tiles
    lat_f = is_lat.astype(F32)
    ctx_f = is_ctx.astype(F32)
    vert_up = jnp.logical_and(is_lat, i > 0).astype(F32)
    vert_dn = jnp.logical_and(is_lat, i < n_lat_tiles - 1).astype(F32)
    hz_prev = jnp.logical_and(is_ctx, ic > 0).astype(F32)
    hz_next = jnp.logical_and(is_ctx, ic < n_ctx_tiles - 1).astype(F32)
    slot = j % 2

    @pl.when(j == 0)
    def _():
        acc_ref[...] = jnp.zeros(acc_ref.shape, F32)
        act_ref[1] = jnp.zeros(act_ref.shape[1:], BF16)

    streams = ((va_ref, vm_ref, vb_ref, (eva_ref, evm0_ref, evm1_ref, evb_ref), kv_ref),
               (ga_ref, gm_ref, gb_ref, (ega_ref, egm0_ref, egm1_ref, egb_ref), kg_ref))
    pair = 2 * b
    zero_row = jnp.zeros((1, b, fb), F32)
    edge = lambda ref, row0, part: ref[row0:row0 + pair, :].astype(F32)[part * b:(part + 1) * b]
    for which, (a_ref, m_ref, b_ref, (ea_ref, em0_ref, em1_ref, eb_ref), k_ref) in enumerate(streams):
        ea_ref[0:1] = zero_row
        ea_ref[w + 1:w + 2] = zero_row
        eb_ref[0:1] = zero_row
        eb_ref[w + 1:w + 2] = zero_row
        em0_ref[0:1] = (edge(a_ref, (w - 2) * b, 1) * hz_prev).reshape(1, b, fb)
        em0_ref[w + 1:w + 2] = (edge(m_ref, w * b, 0) * ctx_f).reshape(1, b, fb)
        em1_ref[0:1] = (edge(m_ref, (w - 2) * b, 1) * ctx_f).reshape(1, b, fb)
        em1_ref[w + 1:w + 2] = (edge(b_ref, 0, 0) * hz_next).reshape(1, b, fb)
        for row, flags in enumerate(((vert_up, None, lat_f), (lat_f, None, vert_dn))):
            for r, flag in enumerate(flags):
                for dc in range(3):
                    kr = k_ref[3 * r + dc:3 * r + dc + 1, :]
                    kr = kr if flag is None else kr * flag
                    kb_ref[18 * which + 9 * row + 3 * r + dc] = jnp.broadcast_to(kr, (b, fb))

    tq = w // FFN_QUARTERS
    nq_lanes = acc_ref.shape[-1] // FFN_QUARTERS
    tp = SUBLANES

    def stage(q):
        lo, hi = q * tq * b, (q + 1) * tq * b
        for a_ref, m_ref, b_ref, (ea_ref, em0_ref, em1_ref, eb_ref), _ in streams:
            for src, off, dst in ((a_ref, 0, ea_ref), (m_ref, 0, em0_ref), (m_ref, w * b, em1_ref),
                                  (b_ref, 0, eb_ref)):
                blk = src[off + lo:off + hi, :].astype(F32)
                dst[1 + q * tq:1 + (q + 1) * tq] = blk.reshape(tq, b, fb)

    def conv_piece(row, t0, lanes, which):
        out = None
        for r, e_ref in enumerate(streams[which][3][row:row + 3]):
            for dc in range(3):
                tap = kb_ref[18 * which + 9 * row + 3 * r + dc, :, lanes]
                term = e_ref[t0 + dc:t0 + dc + tp, :, lanes] * tap
                out = term if out is None else out + term
        return out

    stage(0)
    for q in range(FFN_QUARTERS):
        if q + 1 < FFN_QUARTERS:
            stage(q + 1)
        cols = slice(q * nq_lanes, (q + 1) * nq_lanes)
        acc_ref[:, cols] += jnp.dot(act_ref[1 - slot], wd_ref[:, cols], preferred_element_type=F32)
        for row in range(2):
            for lc in range(fb // LANES):
                lanes = slice(lc * LANES, (lc + 1) * LANES)
                for t0 in range(q * tq, (q + 1) * tq, tp):
                    val = conv_piece(row, t0, lanes, 0)
                    gat = conv_piece(row, t0, lanes, 1)
                    act = (val * jax.nn.silu(gat)).reshape(tp * b, LANES).astype(BF16)
                    act_ref[slot, (row * w + t0) * b:(row * w + t0 + tp) * b, lanes] = act

    @pl.when(j == nf)
    def _():
        o_ref[...] = x_ref[...] + gate_ref[...] * _rms(acc_ref[...], gp_ref[...]).reshape(o_ref.shape)


def _ffn_down(z, w_conv9, w_down, xs, mod, g_post, fb, n_tiles, n_lat_tiles, n_ctx_tiles):
    s, b, d = xs.shape
    f = w_down.shape[0]
    nf = f // fb
    tt = 2 * GRID_W
    rows = tt * b
    half = rows // 2
    up = lambda i: jnp.maximum(2 * i - 1, 0)
    dn = lambda i: jnp.minimum(2 * i + 2, 2 * n_tiles - 1)
    sel = lambda i: jnp.where(i >= n_lat_tiles, 1, 0)
    cur = lambda j: jnp.minimum(j, nf - 1)
    prv = lambda j: jnp.maximum(j - 1, 0)
    zspec = lambda n_rows, row, col0: pl.BlockSpec((None, n_rows, fb),
                                                   lambda i, j: (col0 + cur(j), row(i), 0))
    same = lambda i: i
    assert GRID_W % (FFN_QUARTERS * SUBLANES) == 0 and d % (FFN_QUARTERS * LANES) == 0
    return pl.pallas_call(
        functools.partial(_ffn_down_kernel, n_lat_tiles=n_lat_tiles, n_ctx_tiles=n_ctx_tiles),
        grid=(n_tiles, nf + 1),
        in_specs=[
            zspec(half, up, 0), zspec(rows, same, 0), zspec(half, dn, 0),
            zspec(half, up, nf), zspec(rows, same, nf), zspec(half, dn, nf),
            pl.BlockSpec((9, fb), lambda i, j: (0, cur(j))),
            pl.BlockSpec((9, fb), lambda i, j: (0, nf + cur(j))),
            pl.BlockSpec((fb, d), lambda i, j: (prv(j), 0)),
            pl.BlockSpec((tt, b, d), lambda i, j: (i, 0, 0)),
            pl.BlockSpec((None, b, d), lambda i, j: (sel(i), 0, 5)),
            _const_spec((1, d)),
        ],
        out_specs=pl.BlockSpec((tt, b, d), lambda i, j: (i, 0, 0)),
        out_shape=jax.ShapeDtypeStruct(xs.shape, F32),
        scratch_shapes=[
            pltpu.VMEM((rows, d), F32),
            pltpu.VMEM((2, rows, fb), BF16),
            pltpu.VMEM((36, b, fb), F32),
        ] + [pltpu.VMEM((GRID_W + 2, b, fb), F32)] * 8,
        input_output_aliases={9: 0},
        compiler_params=_params("arbitrary", "arbitrary"),
        name="ffn_down",
    )(z, z, z, z, z, z, w_conv9, w_conv9, w_down, xs, mod, g_post)


def _pick(limit, total):
    t = min(limit, total)
    while total % t:
        t -= 1
    return t


def kernel(x, c, ctx, c_ctx, w_ada, b_ada, w_in, w_pool, pool_scale, ssm_a_re, ssm_a_im, ssm_log_dt,
           ssm_b_re, ssm_b_im, ssm_c_re, ssm_c_im, ssm_d, w_glu, w_out, g_pre_mix, g_post_mix,
           g_pre_ffn, g_post_ffn, w_up, w_conv, w_down):
    bsz, seq, d = x.shape
    n_ctx = ctx.shape[1]
    depth = w_ada.shape[0]
    mixw = w_in.shape[-1]
    sw = ssm_d.shape[-1]
    f = w_down.shape[1]
    assert bsz == SUBLANES and seq % GRID_W == 0 and n_ctx % GRID_W == 0
    assert sw % LANES == 0 and (mixw - sw) % sw == 0 and LANES % ssm_b_re.shape[-1] == 0
    s_all = seq + n_ctx
    tt = GRID_W
    n_lat_tiles = seq // tt
    n_ctx_tiles = n_ctx // tt
    n_tiles = n_lat_tiles + n_ctx_tiles
    tt_up = 2 * GRID_W
    assert seq % tt_up == 0 and n_ctx % tt_up == 0
    nb_up = _pick(1024, 2 * f)
    fb = _pick(256, f)
    lat_chunks = _pick(64, seq // SSM_CHUNK)
    ctx_chunks = _pick(64, n_ctx // SSM_CHUNK)
    assert (seq // SSM_CHUNK) % ctx_chunks == 0

    xs = jnp.concatenate([x, ctx], axis=1).transpose(1, 0, 2)
    c16 = jnp.concatenate([c, jnp.broadcast_to(c_ctx[None, :], (SUBLANES, d))], axis=0)
    mod_all = _ada_all(c16, w_ada, b_ada, _pick(1024, w_ada.shape[-1]))

    row = lambda v: v.reshape(1, -1).astype(F32)
    for l in range(depth):
        last = l == depth - 1
        mod = mod_all[l]
        mats = [_ssm_matrices(ssm_a_re[l, dr], ssm_a_im[l, dr], ssm_log_dt[l, dr], ssm_b_re[l, dr],
                              ssm_b_im[l, dr], ssm_c_re[l, dr], ssm_c_im[l, dr], dr == 1)
                for dr in range(2)]
        mats = tuple(jnp.stack(m) for m in zip(*mats))

        u = _premix(xs, mod, row(g_pre_mix[l]), w_in[l].astype(BF16), tt, n_lat_tiles)
        u5 = u.reshape(s_all // SSM_CHUNK, SSM_CHUNK, bsz, mixw)
        nbk = sw // LANES
        sb = mats[0].shape[-1]
        h0 = jnp.zeros((2, nbk, bsz, sb), F32)
        lane0 = (mixw - sw) // LANES
        y_ctx, h_ctx = _ssm(u5, mats, h0, ctx_chunks, (seq // SSM_CHUNK) // ctx_chunks,
                            (n_ctx // SSM_CHUNK) // ctx_chunks, lane0)
        y_lat, _ = _ssm(u5, mats, h_ctx, lat_chunks, 0, (seq // SSM_CHUNK) // lat_chunks, lane0)

        nt = n_lat_tiles if last else n_tiles
        xs = _mix(u, y_lat.reshape(2, seq, bsz, sw), y_ctx.reshape(2, n_ctx, bsz, sw), xs, mod,
                  w_pool[l].astype(BF16), row(pool_scale[l]), row(ssm_d[l]), w_glu[l].astype(BF16),
                  w_out[l].astype(BF16), row(g_post_mix[l]), tt, nt, n_lat_tiles, seq, n_ctx)
        nt_up = (seq if last else s_all) // tt_up
        z = _ffn_up(xs, mod, row(g_pre_ffn[l]), w_up[l].astype(BF16), tt_up, nb_up, fb, nt_up,
                    seq // tt_up)
        xs = _ffn_down(z, w_conv[l].reshape(9, 2 * f).astype(F32), w_down[l].astype(BF16), xs, mod,
                       row(g_post_ffn[l]), fb, nt_up, seq // tt_up, n_ctx // tt_up)
    return xs[:seq].transpose(1, 0, 2)
```

```python
import functools

import jax
import jax.numpy as jnp
from jax import lax
from jax.experimental import pallas as pl
from jax.experimental.pallas import tpu as pltpu

GRID_W = 64
POOL_WINDOWS = (2, 4, 8, 16)
POOL_HALO = max(POOL_WINDOWS) // 2
EPS = 1e-6
SUBLANES = 8
LANES = 128
SSM_CHUNK = 8
VMEM_LIMIT_BYTES = 62 * 1024 * 1024

F32 = jnp.float32
BF16 = jnp.bfloat16


def _params(*sem):
    return pltpu.CompilerParams(dimension_semantics=sem, vmem_limit_bytes=VMEM_LIMIT_BYTES)


def _const_spec(shape, layer=None):
    zeros = (0,) * len(shape)
    if layer is None:
        return pl.BlockSpec(shape, lambda *_: zeros, pipeline_mode=pl.Buffered(1))
    return pl.BlockSpec((None,) + tuple(shape), lambda *_: (layer,) + zeros, pipeline_mode=pl.Buffered(1))


def _rms(v, gain):
    ms = jnp.mean(v * v, axis=-1, keepdims=True)
    return v * lax.rsqrt(ms + EPS) * gain


def _ada_kernel(c_ref, w_ref, b_ref, o_ref):
    s = jax.nn.silu(c_ref[...]).astype(BF16)
    r = jnp.dot(s, w_ref[...].astype(BF16), preferred_element_type=F32) + b_ref[...]
    o_ref[...] = r.reshape(o_ref.shape)


def _ada_all(c16, w_ada, b_ada, nb):
    depth, d, n = w_ada.shape
    return pl.pallas_call(
        _ada_kernel,
        grid=(depth, n // nb),
        in_specs=[
            pl.BlockSpec((2 * SUBLANES, d), lambda l, j: (0, 0)),
            pl.BlockSpec((None, d, nb), lambda l, j: (l, 0, j)),
            pl.BlockSpec((None, 1, nb), lambda l, j: (l, 0, j)),
        ],
        out_specs=pl.BlockSpec((None, 2, SUBLANES, nb), lambda l, j: (l, 0, 0, j)),
        out_shape=jax.ShapeDtypeStruct((depth, 2, SUBLANES, n), F32),
        compiler_params=_params("arbitrary", "arbitrary"),
        name="ada",
    )(c16, w_ada, b_ada.reshape(depth, 1, n))


def _premix_kernel(x_ref, sh_ref, sc_ref, g_ref, w_ref, u_ref):
    tt, b, d = x_ref.shape
    h = _rms(x_ref[...], g_ref[...]) * (1.0 + sc_ref[...]) + sh_ref[...]
    u = jnp.dot(h.reshape(tt * b, d).astype(BF16), w_ref[...], preferred_element_type=F32)
    u_ref[...] = u.reshape(u_ref.shape)


def _premix_first_kernel(xl_ref, xc_ref, sh_ref, sc_ref, g_ref, w_ref, xs_ref, u_ref, *, n_lat_tiles):
    tt = xs_ref.shape[0]
    is_ctx = pl.program_id(0) >= n_lat_tiles

    @pl.when(is_ctx)
    def _():
        for t in range(tt):
            xs_ref[t] = xc_ref[:, t, :]

    @pl.when(jnp.logical_not(is_ctx))
    def _():
        for t in range(tt):
            xs_ref[t] = xl_ref[:, t, :]

    _premix_kernel(xs_ref, sh_ref, sc_ref, g_ref, w_ref, u_ref)


def _premix_first(x, ctx, mod, g, w, layer, tt):
    b, seq, d = x.shape
    n_ctx = ctx.shape[1]
    n = w.shape[-1]
    n_lat_tiles = seq // tt
    sel = lambda i: jnp.where(i >= n_lat_tiles, 1, 0)
    stream = jax.ShapeDtypeStruct((seq + n_ctx, b, d), F32)
    return pl.pallas_call(
        functools.partial(_premix_first_kernel, n_lat_tiles=n_lat_tiles),
        grid=((seq + n_ctx) // tt,),
        in_specs=[
            pl.BlockSpec((b, tt, d), lambda i: (0, jnp.minimum(i, n_lat_tiles - 1), 0)),
            pl.BlockSpec((b, tt, d), lambda i: (0, jnp.maximum(i - n_lat_tiles, 0), 0)),
            pl.BlockSpec((None, None, b, d), lambda i: (layer, sel(i), 0, 0)),
            pl.BlockSpec((None, None, b, d), lambda i: (layer, sel(i), 0, 1)),
            _const_spec((1, d)),
            _const_spec((d, n), layer),
        ],
        out_specs=[pl.BlockSpec((tt, b, d), lambda i: (i, 0, 0)), pl.BlockSpec((tt, b, n), lambda i: (i, 0, 0))],
        out_shape=[stream, jax.ShapeDtypeStruct((seq + n_ctx, b, n), F32)],
        compiler_params=_params("arbitrary"),
        name="premix_first",
    )(x, ctx, mod, mod, g, w)


def _premix(xs, mod, g, w, layer, tt, n_lat_tiles):
    s, b, d = xs.shape
    n = w.shape[-1]
    sel = lambda i: jnp.where(i >= n_lat_tiles, 1, 0)
    return pl.pallas_call(
        _premix_kernel,
        grid=(s // tt,),
        in_specs=[
            pl.BlockSpec((tt, b, d), lambda i: (i, 0, 0)),
            pl.BlockSpec((None, None, b, d), lambda i: (layer, sel(i), 0, 0)),
            pl.BlockSpec((None, None, b, d), lambda i: (layer, sel(i), 0, 1)),
            _const_spec((1, d)),
            _const_spec((d, n), layer),
        ],
        out_specs=pl.BlockSpec((tt, b, n), lambda i: (i, 0, 0)),
        out_shape=jax.ShapeDtypeStruct((s, b, n), F32),
        compiler_params=_params("arbitrary"),
        name="premix",
    )(xs, mod, mod, g, w)


def _ssm_kernel(u_ref, mo_ref, my_ref, lam_ref, h0_ref, y_ref, hf_ref, s_ref, hin_ref, hc_ref, *,
                chunks):
    c_n = chunks
    sb = hc_ref.shape[-1]
    half = sb // 2
    d = pl.program_id(0)
    t = pl.program_id(2)

    @pl.when(t == 0)
    def _():
        hc_ref[...] = h0_ref[...]

    u_rows = jnp.concatenate(
        [u_ref[:, k].reshape(c_n * SUBLANES, LANES).astype(BF16) for k in range(SSM_CHUNK)], axis=-1)
    s_ref[...] = jnp.dot(u_rows, mo_ref[...], preferred_element_type=F32).reshape(s_ref.shape)

    lre = jnp.broadcast_to(lam_ref[0:1, :], (SUBLANES, half))
    lim = jnp.broadcast_to(lam_ref[1:2, :], (SUBLANES, half))

    def step(i, carry):
        hr, hi = carry
        c = jnp.where(d == 0, i, c_n - 1 - i)
        hin_ref[c, :, :half] = hr
        hin_ref[c, :, half:] = hi
        sr = s_ref[c, :, :half]
        si = s_ref[c, :, half:]
        return lre * hr - lim * hi + sr, lre * hi + lim * hr + si

    hr, hi = lax.fori_loop(0, c_n, step, (hc_ref[:, :half], hc_ref[:, half:]))
    hc_ref[:, :half] = hr
    hc_ref[:, half:] = hi
    hf_ref[:, :half] = hr
    hf_ref[:, half:] = hi

    h_rows = hin_ref[...].reshape(c_n * SUBLANES, sb).astype(BF16)
    y = jnp.dot(jnp.concatenate([u_rows, h_rows], axis=-1), my_ref[...], preferred_element_type=F32)
    for k in range(SSM_CHUNK):
        y_ref[:, k] = y[:, k * LANES:(k + 1) * LANES].reshape(c_n, SUBLANES, LANES)


def _ssm(u5, mats, layer, h0, chunks, block_off, n_tiles, ssm_lane_block0):
    m_out, m_y, lam8 = mats
    b = u5.shape[2]
    _, _, nb, kk, sb = m_out.shape
    tile = lambda d, t: jnp.where(d == 0, t, n_tiles - 1 - t)
    y_shape = (2, n_tiles * chunks, SSM_CHUNK, b, nb * LANES)
    return pl.pallas_call(
        functools.partial(_ssm_kernel, chunks=chunks),
        grid=(2, nb, n_tiles),
        in_specs=[
            pl.BlockSpec((chunks, SSM_CHUNK, b, LANES),
                         lambda d, j, t: (tile(d, t) + block_off, 0, 0, ssm_lane_block0 + j)),
            pl.BlockSpec((None, None, None, kk, sb), lambda d, j, t: (layer, d, j, 0, 0)),
            pl.BlockSpec((None, None, None, kk + sb, kk), lambda d, j, t: (layer, d, j, 0, 0)),
            pl.BlockSpec((None, None, None, 2, sb // 2), lambda d, j, t: (layer, d, j, 0, 0)),
            pl.BlockSpec((None, None, b, sb), lambda d, j, t: (d, j, 0, 0)),
        ],
        out_specs=[
            pl.BlockSpec((None, chunks, SSM_CHUNK, b, LANES), lambda d, j, t: (d, tile(d, t), 0, 0, j)),
            pl.BlockSpec((None, None, b, sb), lambda d, j, t: (d, j, 0, 0)),
        ],
        out_shape=[jax.ShapeDtypeStruct(y_shape, F32), jax.ShapeDtypeStruct(h0.shape, F32)],
        scratch_shapes=[
            pltpu.VMEM((chunks, b, sb), F32),
            pltpu.VMEM((chunks, b, sb), F32),
            pltpu.VMEM((b, sb), F32),
        ],
        compiler_params=_params("arbitrary", "arbitrary", "arbitrary"),
        name="ssm",
    )(u5, m_out, m_y, lam8, h0)


def _ssm_matrices(a_re, a_im, log_dt, b_re, b_im, c_re, c_im, reverse):
    g, p = a_re.shape
    h = b_re.shape[-1]
    t_n = SSM_CHUNK
    gpb = LANES // h
    nb = g // gpb
    a_re = a_re.astype(F32)
    a_im = a_im.astype(F32)
    dt = jnp.exp(log_dt.astype(F32))[:, None]
    mag = jnp.exp(a_re * dt)
    lam_re = mag * jnp.cos(a_im * dt)
    lam_im = mag * jnp.sin(a_im * dt)
    denom = a_re * a_re + a_im * a_im
    nr, ni = lam_re - 1.0, lam_im
    f_re = ((nr * a_re + ni * a_im) / denom)[..., None]
    f_im = ((ni * a_re - nr * a_im) / denom)[..., None]
    b_re = b_re.astype(F32)
    b_im = b_im.astype(F32)
    bb_re = f_re * b_re - f_im * b_im
    bb_im = f_re * b_im + f_im * b_re
    c_re = c_re.astype(F32)
    c_im = c_im.astype(F32)
    pr, pi = [jnp.ones_like(lam_re)], [jnp.zeros_like(lam_im)]
    for _ in range(t_n):
        pr, pi = (pr + [pr[-1] * lam_re - pi[-1] * lam_im], pi + [pr[-1] * lam_im + pi[-1] * lam_re])
    pr = jnp.stack(pr)
    pi = jnp.stack(pi)

    q_re = pr[:t_n, :, :, None] * bb_re - pi[:t_n, :, :, None] * bb_im
    q_im = pr[:t_n, :, :, None] * bb_im + pi[:t_n, :, :, None] * bb_re
    hi = lax.Precision.HIGHEST
    taps = (jnp.einsum('gop,kgpi->kgoi', c_re, q_re, precision=hi)
            - jnp.einsum('gop,kgpi->kgoi', c_im, q_im, precision=hi))
    lag = jnp.arange(t_n)[None, :] - jnp.arange(t_n)[:, None]
    toep = jnp.where((lag >= 0)[:, :, None, None, None], taps[jnp.clip(lag, 0, t_n - 1)], 0.0)
    toep = toep.reshape(t_n, t_n, nb, gpb, h, h)
    cr, ci = c_re[None], c_im[None]
    lr, li = pr[1:, :, None, :], pi[1:, :, None, :]
    w_in = jnp.stack([cr * lr - ci * li, -(cr * li + ci * lr)])
    w_in = w_in.reshape(2, t_n, nb, gpb, h, p)
    w_out = jnp.stack([q_re[::-1], q_im[::-1]]).reshape(2, t_n, nb, gpb, p, h)
    if reverse:
        toep = toep[::-1, ::-1]
        w_in = w_in[:, ::-1]
        w_out = w_out[:, ::-1]
    k_dim = t_n * gpb * h
    sb = 2 * gpb * p

    def on_group_diagonal(compact, row_width, col_width, n_cols):
        n_rows, n_compact = compact.shape[1:]
        col = jnp.arange(n_cols)
        src = (col // (gpb * col_width)) * col_width + col % col_width
        place = (jnp.arange(n_compact)[:, None] == src[None, :]).astype(F32)
        full = jnp.einsum('jrk,kc->jrc', compact, place)
        row_group = (jnp.arange(n_rows) // row_width) % gpb
        col_group = (col // col_width) % gpb
        return jnp.where(row_group[:, None] == col_group[None, :], full, 0.0).astype(BF16)

    m_toep = on_group_diagonal(toep.transpose(2, 0, 3, 5, 1, 4).reshape(nb, k_dim, t_n * h), h, h, k_dim)
    m_in = on_group_diagonal(w_in.transpose(2, 0, 3, 5, 1, 4).reshape(nb, sb, t_n * h), p, h, k_dim)
    m_out = on_group_diagonal(w_out.transpose(2, 1, 3, 5, 0, 4).reshape(nb, k_dim, 2 * p), h, p, sb)
    m_y = jnp.concatenate([m_toep, m_in], axis=1)
    lam8 = jnp.stack([pr[t_n].reshape(nb, gpb * p), pi[t_n].reshape(nb, gpb * p)], axis=1)
    return m_out, m_y, lam8


def _mix_kernel(um_ref, up_ref, un_ref, us_ref, yl_ref, yc_ref, x_ref, gate_ref, wp_ref, ps_ref,
                sd_ref, wg_ref, wo_ref, gp_ref, o_ref, ext_ref, cat_ref, *, n_lat_tiles, seq, ctx):
    tt, b, pw_all = um_ref.shape
    pg = pw_all // len(POOL_WINDOWS)
    sw = us_ref.shape[-1]
    rows = tt * b
    i = pl.program_id(0)
    is_ctx = i >= n_lat_tiles
    t0 = jnp.where(is_ctx, i - n_lat_tiles, i) * tt
    seg = jnp.where(is_ctx, ctx, seq)
    has_prev = (t0 > 0).astype(F32)
    has_next = (t0 + tt < seg).astype(F32)
    hl = POOL_HALO
    ext_ref[0:hl] = up_ref[...] * has_prev
    ext_ref[hl:hl + tt] = um_ref[...]
    ext_ref[hl + tt:hl + tt + hl] = un_ref[...] * has_next

    tl = t0 + lax.broadcasted_iota(jnp.int32, (tt, b, pg), 0)
    for g, w in enumerate(POOL_WINDOWS):
        sl = slice(g * pg, (g + 1) * pg)
        lo = hl - w // 2
        acc = ext_ref[lo:lo + tt, :, sl]
        for k in range(1, w):
            acc = acc + ext_ref[lo + k:lo + k + tt, :, sl]
        cnt = (jnp.minimum(tl + w // 2, seg) - jnp.maximum(tl - w // 2, 0)).astype(F32)
        p = acc / cnt - ext_ref[hl:hl + tt, :, sl]
        po = jnp.dot(p.reshape(rows, pg).astype(BF16), wp_ref[g], preferred_element_type=F32)
        cat_ref[:, sl] = (po * ps_ref[:, sl]).astype(BF16)

    y = jnp.where(is_ctx, yc_ref[0] + yc_ref[1], yl_ref[0] + yl_ref[1])
    yv = y.reshape(rows, sw) + sd_ref[...] * us_ref[...].reshape(rows, sw)
    yv = jax.nn.gelu(yv)
    glu = jax.nn.sigmoid(jnp.dot(yv.astype(BF16), wg_ref[...], preferred_element_type=F32))
    cat_ref[:, pw_all:] = (yv * glu).astype(BF16)

    mix = jnp.dot(cat_ref[...], wo_ref[...], preferred_element_type=F32)
    o_ref[...] = x_ref[...] + gate_ref[...] * _rms(mix, gp_ref[...]).reshape(o_ref.shape)


def _mix(u, y_lat, y_ctx, xs, mod, w_pool, pool_scale, ssm_d, w_glu, w_out, g_post, layer, tt,
         n_tiles, n_lat_tiles, seq, ctx):
    s, b, d = xs.shape
    mixw = u.shape[-1]
    sw = y_lat.shape[-1]
    pw_all = mixw - sw
    hl = POOL_HALO
    hpt = tt // hl
    last_halo = n_tiles * hpt - 1
    sel = lambda i: jnp.where(i >= n_lat_tiles, 1, 0)
    return pl.pallas_call(
        functools.partial(_mix_kernel, n_lat_tiles=n_lat_tiles, seq=seq, ctx=ctx),
        grid=(n_tiles,),
        in_specs=[
            pl.BlockSpec((tt, b, pw_all), lambda i: (i, 0, 0)),
            pl.BlockSpec((hl, b, pw_all), lambda i: (jnp.maximum(i * hpt - 1, 0), 0, 0)),
            pl.BlockSpec((hl, b, pw_all), lambda i: (jnp.minimum((i + 1) * hpt, last_halo), 0, 0)),
            pl.BlockSpec((tt, b, sw), lambda i: (i, 0, pw_all // sw)),
            pl.BlockSpec((2, tt, b, sw), lambda i: (0, jnp.minimum(i, n_lat_tiles - 1), 0, 0)),
            pl.BlockSpec((2, tt, b, sw), lambda i: (0, jnp.maximum(i - n_lat_tiles, 0), 0, 0)),
            pl.BlockSpec((tt, b, d), lambda i: (i, 0, 0)),
            pl.BlockSpec((None, None, b, d), lambda i: (layer, sel(i), 0, 2)),
            _const_spec(w_pool.shape[1:], layer),
            _const_spec((1, pw_all)),
            _const_spec((1, sw)),
            _const_spec(w_glu.shape[1:], layer),
            _const_spec(w_out.shape[1:], layer),
            _const_spec((1, d)),
        ],
        out_specs=pl.BlockSpec((tt, b, d), lambda i: (i, 0, 0)),
        out_shape=jax.ShapeDtypeStruct(xs.shape, F32),
        scratch_shapes=[
            pltpu.VMEM((tt + 2 * hl, b, pw_all), F32),
            pltpu.VMEM((tt * b, mixw), BF16),
        ],
        input_output_aliases={6: 0},
        compiler_params=_params("arbitrary"),
        name="mix",
    )(u, u, u, u, y_lat, y_ctx, xs, mod, w_pool, pool_scale, ssm_d, w_glu, w_out, g_post)


def _ffn_up_kernel(x_ref, sh_ref, sc_ref, g_ref, w_ref, z_ref, h_ref):
    @pl.when(pl.program_id(1) == 0)
    def _():
        h = _rms(x_ref[...], g_ref[...]) * (1.0 + sc_ref[...]) + sh_ref[...]
        h_ref[...] = h.reshape(h_ref.shape).astype(BF16)

    z = jnp.dot(h_ref[...], w_ref[...], preferred_element_type=F32).astype(BF16)
    fb = z_ref.shape[-1]
    for k in range(z_ref.shape[0]):
        z_ref[k] = z[:, k * fb:(k + 1) * fb]


def _ffn_up(xs, mod, g, w_up, layer, tt, nb, fb, n_tiles, n_lat_tiles):
    s, b, d = xs.shape
    n = w_up.shape[-1]
    sel = lambda i: jnp.where(i >= n_lat_tiles, 1, 0)
    return pl.pallas_call(
        _ffn_up_kernel,
        grid=(n_tiles, n // nb),
        in_specs=[
            pl.BlockSpec((tt, b, d), lambda i, j: (i, 0, 0)),
            pl.BlockSpec((None, None, b, d), lambda i, j: (layer, sel(i), 0, 3)),
            pl.BlockSpec((None, None, b, d), lambda i, j: (layer, sel(i), 0, 4)),
            _const_spec((1, d)),
            pl.BlockSpec((None, d, nb), lambda i, j: (layer, 0, j)),
        ],
        out_specs=pl.BlockSpec((nb // fb, tt * b, fb), lambda i, j: (j, i, 0)),
        out_shape=jax.ShapeDtypeStruct((n // fb, n_tiles * tt * b, fb), BF16),
        scratch_shapes=[pltpu.VMEM((tt * b, d), BF16)],
        compiler_params=_params("arbitrary", "arbitrary"),
        name="ffn_up",
    )(xs, mod, mod, g, w_up)


FFN_QUARTERS = 4
CONV_PIECE_STEPS = 4


def _ffn_down_kernel(va_ref, vm_ref, vb_ref, ga_ref, gm_ref, gb_ref, kv_ref, kg_ref, wd_ref, x_ref,
                     gate_ref, gp_ref, o_ref, acc_ref, act_ref, kb_ref, eva_ref, evm0_ref, evm1_ref,
                     evb_ref, ega_ref, egm0_ref, egm1_ref, egb_ref, *, n_lat_tiles, n_ctx_tiles,
                     batch_major_out):
    w, b, fb = evm0_ref.shape[0] - 2, evm0_ref.shape[1], evm0_ref.shape[2]
    i = pl.program_id(0)
    j = pl.program_id(1)
    nf = pl.num_programs(1) - 1
    is_ctx = i >= n_lat_tiles
    is_lat = jnp.logical_not(is_ctx)
    ic = i - n_lat_tiles
    lat_f = is_lat.astype(F32)
    ctx_f = is_ctx.astype(F32)
    vert_up = jnp.logical_and(is_lat, i > 0).astype(F32)
    vert_dn = jnp.logical_and(is_lat, i < n_lat_tiles - 1).astype(F32)
    hz_prev = jnp.logical_and(is_ctx, ic > 0).astype(F32)
    hz_next = jnp.logical_and(is_ctx, ic < n_ctx_tiles - 1).astype(F32)
    slot = j % 2

    @pl.when(j == 0)
    def _():
        acc_ref[...] = jnp.zeros(acc_ref.shape, F32)
        act_ref[1] = jnp.zeros(act_ref.shape[1:], BF16)

    streams = ((va_ref, vm_ref, vb_ref, (eva_ref, evm0_ref, evm1_ref, evb_ref), kv_ref),
               (ga_ref, gm_ref, gb_ref, (ega_ref, egm0_ref, egm1_ref, egb_ref), kg_ref))
    pair = 2 * b
    zero_row = jnp.zeros((1, b, fb), F32)
    edge = lambda ref, row0, part: ref[row0:row0 + pair, :].astype(F32)[part * b:(part + 1) * b]
    for which, (a_ref, m_ref, b_ref, (ea_ref, em0_ref, em1_ref, eb_ref), k_ref) in enumerate(streams):
        ea_ref[0:1] = zero_row
        ea_ref[w + 1:w + 2] = zero_row
        eb_ref[0:1] = zero_row
        eb_ref[w + 1:w + 2] = zero_row
        em0_ref[0:1] = (edge(a_ref, (w - 2) * b, 1) * hz_prev).reshape(1, b, fb)
        em0_ref[w + 1:w + 2] = (edge(m_ref, w * b, 0) * ctx_f).reshape(1, b, fb)
        em1_ref[0:1] = (edge(m_ref, (w - 2) * b, 1) * ctx_f).reshape(1, b, fb)
        em1_ref[w + 1:w + 2] = (edge(b_ref, 0, 0) * hz_next).reshape(1, b, fb)
        for row, flags in enumerate(((vert_up, None, lat_f), (lat_f, None, vert_dn))):
            for r, flag in enumerate(flags):
                for dc in range(3):
                    kr = k_ref[3 * r + dc:3 * r + dc + 1, :]
                    kr = kr if flag is None else kr * flag
                    kb_ref[18 * which + 9 * row + 3 * r + dc] = jnp.broadcast_to(kr, (b, fb))

    tq = w // FFN_QUARTERS
    nq_lanes = acc_ref.shape[-1] // FFN_QUARTERS
    tp = CONV_PIECE_STEPS

    def stage(q):
        lo, hi = q * tq * b, (q + 1) * tq * b
        for a_ref, m_ref, b_ref, (ea_ref, em0_ref, em1_ref, eb_ref), _ in streams:
            for src, off, dst in ((a_ref, 0, ea_ref), (m_ref, 0, em0_ref), (m_ref, w * b, em1_ref),
                                  (b_ref, 0, eb_ref)):
                blk = src[off + lo:off + hi, :].astype(F32)
                dst[1 + q * tq:1 + (q + 1) * tq] = blk.reshape(tq, b, fb)

    def conv_piece(row, t0, lanes, which):
        out = None
        for r, e_ref in enumerate(streams[which][3][row:row + 3]):
            for dc in range(3):
                tap = kb_ref[18 * which + 9 * row + 3 * r + dc, :, lanes]
                term = e_ref[t0 + dc:t0 + dc + tp, :, lanes] * tap
                out = term if out is None else out + term
        return out

    stage(0)
    for q in range(FFN_QUARTERS):
        if q + 1 < FFN_QUARTERS:
            stage(q + 1)
        cols = slice(q * nq_lanes, (q + 1) * nq_lanes)
        acc_ref[:, cols] += jnp.dot(act_ref[1 - slot], wd_ref[:, cols], preferred_element_type=F32)
        for row in range(2):
            for lc in range(fb // LANES):
                lanes = slice(lc * LANES, (lc + 1) * LANES)
                for t0 in range(q * tq, (q + 1) * tq, tp):
                    val = conv_piece(row, t0, lanes, 0)
                    gat = conv_piece(row, t0, lanes, 1)
                    act = (val * jax.nn.silu(gat)).reshape(tp * b, LANES).astype(BF16)
                    act_ref[slot, (row * w + t0) * b:(row * w + t0 + tp) * b, lanes] = act

    @pl.when(j == nf)
    def _():
        out = x_ref[...] + gate_ref[...] * _rms(acc_ref[...], gp_ref[...]).reshape(x_ref.shape)
        if batch_major_out:
            for t in range(out.shape[0]):
                o_ref[:, t, :] = out[t]
        else:
            o_ref[...] = out


def _ffn_down(z, w_conv9, w_down, xs, mod, g_post, layer, fb, n_tiles, n_lat_tiles, n_ctx_tiles,
              batch_major_out):
    s, b, d = xs.shape
    f = w_down.shape[1]
    nf = f // fb
    tt = 2 * GRID_W
    rows = tt * b
    half = rows // 2
    up = lambda i: jnp.maximum(2 * i - 1, 0)
    dn = lambda i: jnp.minimum(2 * i + 2, 2 * n_tiles - 1)
    sel = lambda i: jnp.where(i >= n_lat_tiles, 1, 0)
    cur = lambda j: jnp.minimum(j, nf - 1)
    prv = lambda j: jnp.maximum(j - 1, 0)
    zspec = lambda n_rows, row, col0: pl.BlockSpec((None, n_rows, fb),
                                                   lambda i, j: (col0 + cur(j), row(i), 0))
    same = lambda i: i
    assert GRID_W % (FFN_QUARTERS * SUBLANES) == 0 and d % (FFN_QUARTERS * LANES) == 0
    return pl.pallas_call(
        functools.partial(_ffn_down_kernel, n_lat_tiles=n_lat_tiles, n_ctx_tiles=n_ctx_tiles,
                          batch_major_out=batch_major_out),
        grid=(n_tiles, nf + 1),
        in_specs=[
            zspec(half, up, 0), zspec(rows, same, 0), zspec(half, dn, 0),
            zspec(half, up, nf), zspec(rows, same, nf), zspec(half, dn, nf),
            pl.BlockSpec((9, fb), lambda i, j: (0, cur(j))),
            pl.BlockSpec((9, fb), lambda i, j: (0, nf + cur(j))),
            pl.BlockSpec((None, fb, d), lambda i, j: (layer, prv(j), 0)),
            pl.BlockSpec((tt, b, d), lambda i, j: (i, 0, 0)),
            pl.BlockSpec((None, None, b, d), lambda i, j: (layer, sel(i), 0, 5)),
            _const_spec((1, d)),
        ],
        out_specs=(pl.BlockSpec((b, tt, d), lambda i, j: (0, i, 0)) if batch_major_out
                   else pl.BlockSpec((tt, b, d), lambda i, j: (i, 0, 0))),
        out_shape=jax.ShapeDtypeStruct((b, n_tiles * tt, d) if batch_major_out else xs.shape, F32),
        scratch_shapes=[
            pltpu.VMEM((rows, d), F32),
            pltpu.VMEM((2, rows, fb), BF16),
            pltpu.VMEM((36, b, fb), F32),
        ] + [pltpu.VMEM((GRID_W + 2, b, fb), F32)] * 8,
        input_output_aliases={} if batch_major_out else {9: 0},
        compiler_params=_params("arbitrary", "arbitrary"),
        name="ffn_down",
    )(z, z, z, z, z, z, w_conv9, w_conv9, w_down, xs, mod, g_post)


def _pick(limit, total):
    t = min(limit, total)
    while total % t:
        t -= 1
    return t


def kernel(x, c, ctx, c_ctx, w_ada, b_ada, w_in, w_pool, pool_scale, ssm_a_re, ssm_a_im, ssm_log_dt,
           ssm_b_re, ssm_b_im, ssm_c_re, ssm_c_im, ssm_d, w_glu, w_out, g_pre_mix, g_post_mix,
           g_pre_ffn, g_post_ffn, w_up, w_conv, w_down):
    bsz, seq, d = x.shape
    n_ctx = ctx.shape[1]
    depth = w_ada.shape[0]
    mixw = w_in.shape[-1]
    sw = ssm_d.shape[-1]
    f = w_down.shape[1]
    assert bsz == SUBLANES and seq % GRID_W == 0 and n_ctx % GRID_W == 0
    assert sw % LANES == 0 and (mixw - sw) % sw == 0 and LANES % ssm_b_re.shape[-1] == 0
    s_all = seq + n_ctx
    tt = GRID_W
    n_lat_tiles = seq // tt
    n_ctx_tiles = n_ctx // tt
    n_tiles = n_lat_tiles + n_ctx_tiles
    tt_up = 2 * GRID_W
    assert seq % tt_up == 0 and n_ctx % tt_up == 0
    nb_up = _pick(1024, 2 * f)
    fb = _pick(256, f)
    lat_chunks = _pick(64, seq // SSM_CHUNK)
    ctx_chunks = _pick(64, n_ctx // SSM_CHUNK)
    assert (seq // SSM_CHUNK) % ctx_chunks == 0

    c16 = jnp.concatenate([c, jnp.broadcast_to(c_ctx[None, :], (SUBLANES, d))], axis=0)
    mod_all = _ada_all(c16, w_ada, b_ada, _pick(1024, w_ada.shape[-1]))

    row = lambda v: v.reshape(1, -1).astype(F32)
    w_in, w_pool, w_glu, w_out, w_up, w_down = (v.astype(BF16) for v in (w_in, w_pool, w_glu, w_out,
                                                                          w_up, w_down))
    ssm_params = (ssm_a_re, ssm_a_im, ssm_log_dt, ssm_b_re, ssm_b_im, ssm_c_re, ssm_c_im)
    mats = [jax.vmap(functools.partial(_ssm_matrices, reverse=dr == 1))(*(v[:, dr] for v in ssm_params))
            for dr in range(2)]
    mats = tuple(jnp.stack(m, axis=1) for m in zip(*mats))
    nbk = sw // LANES
    lane0 = (mixw - sw) // LANES
    h0 = jnp.zeros((2, nbk, bsz, mats[0].shape[-1]), F32)
    for l in range(depth):
        last = l == depth - 1
        if l == 0:
            xs, u = _premix_first(x, ctx, mod_all, row(g_pre_mix[l]), w_in, l, tt)
        else:
            u = _premix(xs, mod_all, row(g_pre_mix[l]), w_in, l, tt, n_lat_tiles)
        u5 = u.reshape(s_all // SSM_CHUNK, SSM_CHUNK, bsz, mixw)
        y_ctx, h_ctx = _ssm(u5, mats, l, h0, ctx_chunks, (seq // SSM_CHUNK) // ctx_chunks,
                            (n_ctx // SSM_CHUNK) // ctx_chunks, lane0)
        y_lat, _ = _ssm(u5, mats, l, h_ctx, lat_chunks, 0, (seq // SSM_CHUNK) // lat_chunks, lane0)

        nt = n_lat_tiles if last else n_tiles
        xs = _mix(u, y_lat.reshape(2, seq, bsz, sw), y_ctx.reshape(2, n_ctx, bsz, sw), xs, mod_all,
                  w_pool, row(pool_scale[l]), row(ssm_d[l]), w_glu, w_out, row(g_post_mix[l]), l, tt,
                  nt, n_lat_tiles, seq, n_ctx)
        nt_up = (seq if last else s_all) // tt_up
        z = _ffn_up(xs, mod_all, row(g_pre_ffn[l]), w_up, l, tt_up, nb_up, fb, nt_up, seq // tt_up)
        xs = _ffn_down(z, w_conv[l].reshape(9, 2 * f).astype(F32), w_down, xs, mod_all,
                       row(g_post_ffn[l]), l, fb, nt_up, seq // tt_up, n_ctx // tt_up, last)
    return xs
```

```python
import functools
import math

import jax
import jax.numpy as jnp
from jax import lax
from jax.experimental import pallas as pl
from jax.experimental.pallas import tpu as pltpu

GRID_W = 64
POOL_WINDOWS = (2, 4, 8, 16)
POOL_HALO = max(POOL_WINDOWS) // 2
EPS = 1e-6
SUBLANES = 8
LANES = 128
SSM_CHUNK = 8
VMEM_LIMIT_BYTES = 62 * 1024 * 1024

F32 = jnp.float32
BF16 = jnp.bfloat16


def _params(*sem):
    return pltpu.CompilerParams(dimension_semantics=sem, vmem_limit_bytes=VMEM_LIMIT_BYTES)


def _const_spec(shape, layer=None):
    zeros = (0,) * len(shape)
    if layer is None:
        return pl.BlockSpec(shape, lambda *_: zeros, pipeline_mode=pl.Buffered(1))
    return pl.BlockSpec((None,) + tuple(shape), lambda *_: (layer,) + zeros, pipeline_mode=pl.Buffered(1))


def _rms(v, gain):
    ms = jnp.mean(v * v, axis=-1, keepdims=True)
    return v * lax.rsqrt(ms + EPS) * gain


def _ada_kernel(c_ref, w_ref, b_ref, o_ref):
    s = jax.nn.silu(c_ref[...]).astype(BF16)
    r = jnp.dot(s, w_ref[...].astype(BF16), preferred_element_type=F32) + b_ref[...]
    o_ref[...] = r.reshape(o_ref.shape)


def _ada_all(c16, w_ada, b_ada, nb):
    depth, d, n = w_ada.shape
    return pl.pallas_call(
        _ada_kernel,
        grid=(depth, n // nb),
        in_specs=[
            pl.BlockSpec((2 * SUBLANES, d), lambda l, j: (0, 0)),
            pl.BlockSpec((None, d, nb), lambda l, j: (l, 0, j)),
            pl.BlockSpec((None, 1, nb), lambda l, j: (l, 0, j)),
        ],
        out_specs=pl.BlockSpec((None, 2, SUBLANES, nb), lambda l, j: (l, 0, 0, j)),
        out_shape=jax.ShapeDtypeStruct((depth, 2, SUBLANES, n), F32),
        compiler_params=_params("arbitrary", "arbitrary"),
        name="ada",
    )(c16, w_ada, b_ada.reshape(depth, 1, n))


def _premix_kernel(x_ref, sh_ref, sc_ref, g_ref, w_ref, u_ref):
    tt, b, d = x_ref.shape
    h = _rms(x_ref[...], g_ref[...]) * (1.0 + sc_ref[...]) + sh_ref[...]
    u = jnp.dot(h.reshape(tt * b, d).astype(BF16), w_ref[...], preferred_element_type=F32)
    u_ref[...] = u.reshape(u_ref.shape)


def _premix_first_kernel(xl_ref, xc_ref, sh_ref, sc_ref, g_ref, w_ref, xs_ref, u_ref, *, n_lat_tiles):
    tt = xs_ref.shape[0]
    is_ctx = pl.program_id(0) >= n_lat_tiles

    @pl.when(is_ctx)
    def _():
        for t in range(tt):
            xs_ref[t] = xc_ref[:, t, :]

    @pl.when(jnp.logical_not(is_ctx))
    def _():
        for t in range(tt):
            xs_ref[t] = xl_ref[:, t, :]

    _premix_kernel(xs_ref, sh_ref, sc_ref, g_ref, w_ref, u_ref)


def _premix_first(x, ctx, mod, g, w, layer, tt):
    b, seq, d = x.shape
    n_ctx = ctx.shape[1]
    n = w.shape[-1]
    n_lat_tiles = seq // tt
    sel = lambda i: jnp.where(i >= n_lat_tiles, 1, 0)
    stream = jax.ShapeDtypeStruct((seq + n_ctx, b, d), F32)
    return pl.pallas_call(
        functools.partial(_premix_first_kernel, n_lat_tiles=n_lat_tiles),
        grid=((seq + n_ctx) // tt,),
        in_specs=[
            pl.BlockSpec((b, tt, d), lambda i: (0, jnp.minimum(i, n_lat_tiles - 1), 0)),
            pl.BlockSpec((b, tt, d), lambda i: (0, jnp.maximum(i - n_lat_tiles, 0), 0)),
            pl.BlockSpec((None, None, b, d), lambda i: (layer, sel(i), 0, 0)),
            pl.BlockSpec((None, None, b, d), lambda i: (layer, sel(i), 0, 1)),
            _const_spec((1, d)),
            _const_spec((d, n), layer),
        ],
        out_specs=[pl.BlockSpec((tt, b, d), lambda i: (i, 0, 0)), pl.BlockSpec((tt, b, n), lambda i: (i, 0, 0))],
        out_shape=[stream, jax.ShapeDtypeStruct((seq + n_ctx, b, n), F32)],
        compiler_params=_params("arbitrary"),
        name="premix_first",
    )(x, ctx, mod, mod, g, w)


def _premix(xs, mod, g, w, layer, tt, n_lat_tiles):
    s, b, d = xs.shape
    n = w.shape[-1]
    sel = lambda i: jnp.where(i >= n_lat_tiles, 1, 0)
    return pl.pallas_call(
        _premix_kernel,
        grid=(s // tt,),
        in_specs=[
            pl.BlockSpec((tt, b, d), lambda i: (i, 0, 0)),
            pl.BlockSpec((None, None, b, d), lambda i: (layer, sel(i), 0, 0)),
            pl.BlockSpec((None, None, b, d), lambda i: (layer, sel(i), 0, 1)),
            _const_spec((1, d)),
            _const_spec((d, n), layer),
        ],
        out_specs=pl.BlockSpec((tt, b, n), lambda i: (i, 0, 0)),
        out_shape=jax.ShapeDtypeStruct((s, b, n), F32),
        compiler_params=_params("arbitrary"),
        name="premix",
    )(xs, mod, mod, g, w)


def _ssm_kernel(u_ref, mo_ref, my_ref, lam_ref, h0_ref, y_ref, hf_ref, s_ref, hin_ref, hc_ref, *,
                chunks):
    c_n = chunks
    sb = hc_ref.shape[-1]
    half = sb // 2
    d = pl.program_id(0)
    t = pl.program_id(2)

    @pl.when(t == 0)
    def _():
        hc_ref[...] = h0_ref[...]

    u_rows = jnp.concatenate(
        [u_ref[:, k].reshape(c_n * SUBLANES, LANES).astype(BF16) for k in range(SSM_CHUNK)], axis=-1)
    s_ref[...] = jnp.dot(u_rows, mo_ref[...], preferred_element_type=F32).reshape(s_ref.shape)

    lre = jnp.broadcast_to(lam_ref[0:1, :], (SUBLANES, half))
    lim = jnp.broadcast_to(lam_ref[1:2, :], (SUBLANES, half))

    def step(i, carry):
        hr, hi = carry
        c = jnp.where(d == 0, i, c_n - 1 - i)
        hin_ref[c, :, :half] = hr
        hin_ref[c, :, half:] = hi
        sr = s_ref[c, :, :half]
        si = s_ref[c, :, half:]
        return lre * hr - lim * hi + sr, lre * hi + lim * hr + si

    hr, hi = lax.fori_loop(0, c_n, step, (hc_ref[:, :half], hc_ref[:, half:]))
    hc_ref[:, :half] = hr
    hc_ref[:, half:] = hi
    hf_ref[:, :half] = hr
    hf_ref[:, half:] = hi

    h_rows = hin_ref[...].reshape(c_n * SUBLANES, sb).astype(BF16)
    y = jnp.dot(jnp.concatenate([u_rows, h_rows], axis=-1), my_ref[...], preferred_element_type=F32)
    for k in range(SSM_CHUNK):
        y_ref[:, k] = y[:, k * LANES:(k + 1) * LANES].reshape(c_n, SUBLANES, LANES)


def _ssm(u5, mats, layer, h0, chunks, block_off, n_tiles, ssm_lane_block0):
    m_out, m_y, lam8 = mats
    b = u5.shape[2]
    _, _, nb, kk, sb = m_out.shape
    tile = lambda d, t: jnp.where(d == 0, t, n_tiles - 1 - t)
    y_shape = (2, n_tiles * chunks, SSM_CHUNK, b, nb * LANES)
    return pl.pallas_call(
        functools.partial(_ssm_kernel, chunks=chunks),
        grid=(2, nb, n_tiles),
        in_specs=[
            pl.BlockSpec((chunks, SSM_CHUNK, b, LANES),
                         lambda d, j, t: (tile(d, t) + block_off, 0, 0, ssm_lane_block0 + j)),
            pl.BlockSpec((None, None, None, kk, sb), lambda d, j, t: (layer, d, j, 0, 0)),
            pl.BlockSpec((None, None, None, kk + sb, kk), lambda d, j, t: (layer, d, j, 0, 0)),
            pl.BlockSpec((None, None, None, 2, sb // 2), lambda d, j, t: (layer, d, j, 0, 0)),
            pl.BlockSpec((None, None, b, sb), lambda d, j, t: (d, j, 0, 0)),
        ],
        out_specs=[
            pl.BlockSpec((None, chunks, SSM_CHUNK, b, LANES), lambda d, j, t: (d, tile(d, t), 0, 0, j)),
            pl.BlockSpec((None, None, b, sb), lambda d, j, t: (d, j, 0, 0)),
        ],
        out_shape=[jax.ShapeDtypeStruct(y_shape, F32), jax.ShapeDtypeStruct(h0.shape, F32)],
        scratch_shapes=[
            pltpu.VMEM((chunks, b, sb), F32),
            pltpu.VMEM((chunks, b, sb), F32),
            pltpu.VMEM((b, sb), F32),
        ],
        compiler_params=_params("arbitrary", "arbitrary", "arbitrary"),
        name="ssm",
    )(u5, m_out, m_y, lam8, h0)


def _ssm_matrices(a_re, a_im, log_dt, b_re, b_im, c_re, c_im, reverse):
    g, p = a_re.shape
    h = b_re.shape[-1]
    t_n = SSM_CHUNK
    gpb = LANES // h
    nb = g // gpb
    a_re = a_re.astype(F32)
    a_im = a_im.astype(F32)
    dt = jnp.exp(log_dt.astype(F32))[:, None]
    mag = jnp.exp(a_re * dt)
    lam_re = mag * jnp.cos(a_im * dt)
    lam_im = mag * jnp.sin(a_im * dt)
    denom = a_re * a_re + a_im * a_im
    nr, ni = lam_re - 1.0, lam_im
    f_re = ((nr * a_re + ni * a_im) / denom)[..., None]
    f_im = ((ni * a_re - nr * a_im) / denom)[..., None]
    b_re = b_re.astype(F32)
    b_im = b_im.astype(F32)
    bb_re = f_re * b_re - f_im * b_im
    bb_im = f_re * b_im + f_im * b_re
    c_re = c_re.astype(F32)
    c_im = c_im.astype(F32)
    pr, pi = [jnp.ones_like(lam_re)], [jnp.zeros_like(lam_im)]
    for _ in range(t_n):
        pr, pi = (pr + [pr[-1] * lam_re - pi[-1] * lam_im], pi + [pr[-1] * lam_im + pi[-1] * lam_re])
    pr = jnp.stack(pr)
    pi = jnp.stack(pi)

    q_re = pr[:t_n, :, :, None] * bb_re - pi[:t_n, :, :, None] * bb_im
    q_im = pr[:t_n, :, :, None] * bb_im + pi[:t_n, :, :, None] * bb_re
    hi = lax.Precision.HIGHEST
    taps = (jnp.einsum('gop,kgpi->kgoi', c_re, q_re, precision=hi)
            - jnp.einsum('gop,kgpi->kgoi', c_im, q_im, precision=hi))
    lag = jnp.arange(t_n)[None, :] - jnp.arange(t_n)[:, None]
    toep = jnp.where((lag >= 0)[:, :, None, None, None], taps[jnp.clip(lag, 0, t_n - 1)], 0.0)
    toep = toep.reshape(t_n, t_n, nb, gpb, h, h)
    cr, ci = c_re[None], c_im[None]
    lr, li = pr[1:, :, None, :], pi[1:, :, None, :]
    w_in = jnp.stack([cr * lr - ci * li, -(cr * li + ci * lr)])
    w_in = w_in.reshape(2, t_n, nb, gpb, h, p)
    w_out = jnp.stack([q_re[::-1], q_im[::-1]]).reshape(2, t_n, nb, gpb, p, h)
    if reverse:
        toep = toep[::-1, ::-1]
        w_in = w_in[:, ::-1]
        w_out = w_out[:, ::-1]
    k_dim = t_n * gpb * h
    sb = 2 * gpb * p

    def on_group_diagonal(compact, row_width, col_width, n_cols):
        n_rows, n_compact = compact.shape[1:]
        col = jnp.arange(n_cols)
        src = (col // (gpb * col_width)) * col_width + col % col_width
        place = (jnp.arange(n_compact)[:, None] == src[None, :]).astype(F32)
        full = jnp.einsum('jrk,kc->jrc', compact, place)
        row_group = (jnp.arange(n_rows) // row_width) % gpb
        col_group = (col // col_width) % gpb
        return jnp.where(row_group[:, None] == col_group[None, :], full, 0.0).astype(BF16)

    m_toep = on_group_diagonal(toep.transpose(2, 0, 3, 5, 1, 4).reshape(nb, k_dim, t_n * h), h, h, k_dim)
    m_in = on_group_diagonal(w_in.transpose(2, 0, 3, 5, 1, 4).reshape(nb, sb, t_n * h), p, h, k_dim)
    m_out = on_group_diagonal(w_out.transpose(2, 1, 3, 5, 0, 4).reshape(nb, k_dim, 2 * p), h, p, sb)
    m_y = jnp.concatenate([m_toep, m_in], axis=1)
    lam8 = jnp.stack([pr[t_n].reshape(nb, gpb * p), pi[t_n].reshape(nb, gpb * p)], axis=1)
    return m_out, m_y, lam8


def _mix_kernel(um_ref, up_ref, un_ref, us_ref, yl_ref, yc_ref, x_ref, gate_ref, sh_ref, sc_ref,
                sd_ref, wg_ref, wo_ref, gp_ref, gf_ref, o_ref, h_ref, ext_ref, cat_ref, *,
                n_lat_tiles, seq, ctx):
    tt, b, pw_all = um_ref.shape
    pg = pw_all // len(POOL_WINDOWS)
    sw = us_ref.shape[-1]
    rows = tt * b
    i = pl.program_id(0)
    is_ctx = i >= n_lat_tiles
    t0 = jnp.where(is_ctx, i - n_lat_tiles, i) * tt
    seg = jnp.where(is_ctx, ctx, seq)
    has_prev = (t0 > 0).astype(F32)
    has_next = (t0 + tt < seg).astype(F32)
    hl = POOL_HALO
    ext_ref[0:hl] = up_ref[...] * has_prev
    ext_ref[hl:hl + tt] = um_ref[...]
    ext_ref[hl + tt:hl + tt + hl] = un_ref[...] * has_next

    tl = t0 + lax.broadcasted_iota(jnp.int32, (tt, b, pg), 0)
    for g, w in enumerate(POOL_WINDOWS):
        sl = slice(g * pg, (g + 1) * pg)
        lo = hl - w // 2
        acc = ext_ref[lo:lo + tt, :, sl]
        for k in range(1, w):
            acc = acc + ext_ref[lo + k:lo + k + tt, :, sl]
        cnt = (jnp.minimum(tl + w // 2, seg) - jnp.maximum(tl - w // 2, 0)).astype(F32)
        p = acc / cnt - ext_ref[hl:hl + tt, :, sl]
        cat_ref[:, sl] = p.reshape(rows, pg).astype(BF16)

    y = jnp.where(is_ctx, yc_ref[0] + yc_ref[1], yl_ref[0] + yl_ref[1])
    yv = y.reshape(rows, sw) + sd_ref[...] * us_ref[...].reshape(rows, sw)
    yv = jax.nn.gelu(yv)
    glu = jax.nn.sigmoid(jnp.dot(yv.astype(BF16), wg_ref[...], preferred_element_type=F32))
    cat_ref[:, pw_all:] = (yv * glu).astype(BF16)

    mix = jnp.dot(cat_ref[...], wo_ref[...], preferred_element_type=F32)
    x_new = x_ref[...] + gate_ref[...] * _rms(mix, gp_ref[...]).reshape(o_ref.shape)
    o_ref[...] = x_new
    h = _rms(x_new, gf_ref[...]) * (1.0 + sc_ref[...]) + sh_ref[...]
    h_ref[...] = h.reshape(h_ref.shape).astype(BF16)


def _mix(u, y_lat, y_ctx, xs, mod, ssm_d, w_glu, w_out, g_post, g_ffn, layer, tt, n_tiles,
         n_lat_tiles, seq, ctx):
    s, b, d = xs.shape
    mixw = u.shape[-1]
    sw = y_lat.shape[-1]
    pw_all = mixw - sw
    hl = POOL_HALO
    hpt = tt // hl
    last_halo = n_tiles * hpt - 1
    sel = lambda i: jnp.where(i >= n_lat_tiles, 1, 0)
    return pl.pallas_call(
        functools.partial(_mix_kernel, n_lat_tiles=n_lat_tiles, seq=seq, ctx=ctx),
        grid=(n_tiles,),
        in_specs=[
            pl.BlockSpec((tt, b, pw_all), lambda i: (i, 0, 0)),
            pl.BlockSpec((hl, b, pw_all), lambda i: (jnp.maximum(i * hpt - 1, 0), 0, 0)),
            pl.BlockSpec((hl, b, pw_all), lambda i: (jnp.minimum((i + 1) * hpt, last_halo), 0, 0)),
            pl.BlockSpec((tt, b, sw), lambda i: (i, 0, pw_all // sw)),
            pl.BlockSpec((2, tt, b, sw), lambda i: (0, jnp.minimum(i, n_lat_tiles - 1), 0, 0)),
            pl.BlockSpec((2, tt, b, sw), lambda i: (0, jnp.maximum(i - n_lat_tiles, 0), 0, 0)),
            pl.BlockSpec((tt, b, d), lambda i: (i, 0, 0)),
            pl.BlockSpec((None, None, b, d), lambda i: (layer, sel(i), 0, 2)),
            pl.BlockSpec((None, None, b, d), lambda i: (layer, sel(i), 0, 3)),
            pl.BlockSpec((None, None, b, d), lambda i: (layer, sel(i), 0, 4)),
            _const_spec((1, sw)),
            _const_spec(w_glu.shape[1:], layer),
            _const_spec(w_out.shape[1:], layer),
            _const_spec((1, d)),
            _const_spec((1, d)),
        ],
        out_specs=[pl.BlockSpec((tt, b, d), lambda i: (i, 0, 0)),
                   pl.BlockSpec((tt * b, d), lambda i: (i, 0))],
        out_shape=[jax.ShapeDtypeStruct(xs.shape, F32),
                   jax.ShapeDtypeStruct((n_tiles * tt * b, d), BF16)],
        scratch_shapes=[
            pltpu.VMEM((tt + 2 * hl, b, pw_all), F32),
            pltpu.VMEM((tt * b, mixw), BF16),
        ],
        input_output_aliases={6: 0},
        compiler_params=_params("arbitrary"),
        name="mix",
    )(u, u, u, u, y_lat, y_ctx, xs, mod, mod, mod, ssm_d, w_glu, w_out, g_post, g_ffn)


def _ffn_up_kernel(h_ref, w_ref, z_ref):
    z = jnp.dot(h_ref[...], w_ref[...], preferred_element_type=F32).astype(BF16)
    fb = z_ref.shape[-1]
    for k in range(z_ref.shape[0]):
        z_ref[k] = z[:, k * fb:(k + 1) * fb]


def _ffn_up(h, w_up, layer, tm, nb, fb):
    rows, d = h.shape
    n = w_up.shape[-1]
    return pl.pallas_call(
        _ffn_up_kernel,
        grid=(rows // tm, n // nb),
        in_specs=[
            pl.BlockSpec((tm, d), lambda i, j: (i, 0)),
            pl.BlockSpec((None, d, nb), lambda i, j: (layer, 0, j)),
        ],
        out_specs=pl.BlockSpec((nb // fb, tm, fb), lambda i, j: (j, i, 0)),
        out_shape=jax.ShapeDtypeStruct((n // fb, rows, fb), BF16),
        compiler_params=_params("arbitrary", "arbitrary"),
        name="ffn_up",
    )(h, w_up)


FFN_QUARTERS = 4
CONV_PIECE_STEPS = 4


def _ffn_down_kernel(va_ref, vm_ref, vb_ref, ga_ref, gm_ref, gb_ref, kv_ref, kg_ref, wd_ref, x_ref,
                     gate_ref, gp_ref, o_ref, acc_ref, act_ref, kb_ref, eva_ref, evm0_ref, evm1_ref,
                     evb_ref, ega_ref, egm0_ref, egm1_ref, egb_ref, *, n_lat_tiles, n_ctx_tiles,
                     batch_major_out):
    w, b, fb = evm0_ref.shape[0] - 2, evm0_ref.shape[1], evm0_ref.shape[2]
    i = pl.program_id(0)
    j = pl.program_id(1)
    nf = pl.num_programs(1) - 1
    is_ctx = i >= n_lat_tiles
    is_lat = jnp.logical_not(is_ctx)
    ic = i - n_lat_tiles
    lat_f = is_lat.astype(F32)
    ctx_f = is_ctx.astype(F32)
    vert_up = jnp.logical_and(is_lat, i > 0).astype(F32)
    vert_dn = jnp.logical_and(is_lat, i < n_lat_tiles - 1).astype(F32)
    hz_prev = jnp.logical_and(is_ctx, ic > 0).astype(F32)
    hz_next = jnp.logical_and(is_ctx, ic < n_ctx_tiles - 1).astype(F32)
    slot = j % 2

    @pl.when(j == 0)
    def _():
        acc_ref[...] = jnp.zeros(acc_ref.shape, F32)
        act_ref[1] = jnp.zeros(act_ref.shape[1:], BF16)

    streams = ((va_ref, vm_ref, vb_ref, (eva_ref, evm0_ref, evm1_ref, evb_ref), kv_ref),
               (ga_ref, gm_ref, gb_ref, (ega_ref, egm0_ref, egm1_ref, egb_ref), kg_ref))
    pair = 2 * b
    zero_row = jnp.zeros((1, b, fb), F32)
    edge = lambda ref, row0, part: ref[row0:row0 + pair, :].astype(F32)[part * b:(part + 1) * b]
    for which, (a_ref, m_ref, b_ref, (ea_ref, em0_ref, em1_ref, eb_ref), k_ref) in enumerate(streams):
        ea_ref[0:1] = zero_row
        ea_ref[w + 1:w + 2] = zero_row
        eb_ref[0:1] = zero_row
        eb_ref[w + 1:w + 2] = zero_row
        em0_ref[0:1] = (edge(a_ref, (w - 2) * b, 1) * hz_prev).reshape(1, b, fb)
        em0_ref[w + 1:w + 2] = (edge(m_ref, w * b, 0) * ctx_f).reshape(1, b, fb)
        em1_ref[0:1] = (edge(m_ref, (w - 2) * b, 1) * ctx_f).reshape(1, b, fb)
        em1_ref[w + 1:w + 2] = (edge(b_ref, 0, 0) * hz_next).reshape(1, b, fb)
        for row, flags in enumerate(((vert_up, None, lat_f), (lat_f, None, vert_dn))):
            for r, flag in enumerate(flags):
                for dc in range(3):
                    kr = k_ref[3 * r + dc:3 * r + dc + 1, :]
                    kr = kr if flag is None else kr * flag
                    kb_ref[18 * which + 9 * row + 3 * r + dc] = jnp.broadcast_to(kr, (b, fb))

    tq = w // FFN_QUARTERS
    nq_lanes = acc_ref.shape[-1] // FFN_QUARTERS
    tp = CONV_PIECE_STEPS

    def stage(q):
        lo, hi = q * tq * b, (q + 1) * tq * b
        for a_ref, m_ref, b_ref, (ea_ref, em0_ref, em1_ref, eb_ref), _ in streams:
            for src, off, dst in ((a_ref, 0, ea_ref), (m_ref, 0, em0_ref), (m_ref, w * b, em1_ref),
                                  (b_ref, 0, eb_ref)):
                blk = src[off + lo:off + hi, :].astype(F32)
                dst[1 + q * tq:1 + (q + 1) * tq] = blk.reshape(tq, b, fb)

    def conv_piece(row, t0, lanes, which):
        out = None
        for r, e_ref in enumerate(streams[which][3][row:row + 3]):
            for dc in range(3):
                tap = kb_ref[18 * which + 9 * row + 3 * r + dc, :, lanes]
                term = e_ref[t0 + dc:t0 + dc + tp, :, lanes] * tap
                out = term if out is None else out + term
        return out

    stage(0)
    for q in range(FFN_QUARTERS):
        if q + 1 < FFN_QUARTERS:
            stage(q + 1)
        cols = slice(q * nq_lanes, (q + 1) * nq_lanes)
        acc_ref[:, cols] += jnp.dot(act_ref[1 - slot], wd_ref[:, cols], preferred_element_type=F32)
        for row in range(2):
            for lc in range(fb // LANES):
                lanes = slice(lc * LANES, (lc + 1) * LANES)
                for t0 in range(q * tq, (q + 1) * tq, tp):
                    val = conv_piece(row, t0, lanes, 0)
                    gat = conv_piece(row, t0, lanes, 1)
                    act = (val * jax.nn.silu(gat)).reshape(tp * b, LANES).astype(BF16)
                    act_ref[slot, (row * w + t0) * b:(row * w + t0 + tp) * b, lanes] = act

    @pl.when(j == nf)
    def _():
        out = x_ref[...] + gate_ref[...] * _rms(acc_ref[...], gp_ref[...]).reshape(x_ref.shape)
        if batch_major_out:
            for t in range(out.shape[0]):
                o_ref[:, t, :] = out[t]
        else:
            o_ref[...] = out


def _ffn_down(z, w_conv9, w_down, xs, mod, g_post, layer, fb, n_tiles, n_lat_tiles, n_ctx_tiles,
              batch_major_out):
    s, b, d = xs.shape
    f = w_down.shape[1]
    nf = f // fb
    tt = 2 * GRID_W
    rows = tt * b
    half = rows // 2
    up = lambda i: jnp.maximum(2 * i - 1, 0)
    dn = lambda i: jnp.minimum(2 * i + 2, 2 * n_tiles - 1)
    sel = lambda i: jnp.where(i >= n_lat_tiles, 1, 0)
    cur = lambda j: jnp.minimum(j, nf - 1)
    prv = lambda j: jnp.maximum(j - 1, 0)
    zspec = lambda n_rows, row, col0: pl.BlockSpec((None, n_rows, fb),
                                                   lambda i, j: (col0 + cur(j), row(i), 0))
    same = lambda i: i
    assert GRID_W % (FFN_QUARTERS * SUBLANES) == 0 and d % (FFN_QUARTERS * LANES) == 0
    return pl.pallas_call(
        functools.partial(_ffn_down_kernel, n_lat_tiles=n_lat_tiles, n_ctx_tiles=n_ctx_tiles,
                          batch_major_out=batch_major_out),
        grid=(n_tiles, nf + 1),
        in_specs=[
            zspec(half, up, 0), zspec(rows, same, 0), zspec(half, dn, 0),
            zspec(half, up, nf), zspec(rows, same, nf), zspec(half, dn, nf),
            pl.BlockSpec((9, fb), lambda i, j: (0, cur(j))),
            pl.BlockSpec((9, fb), lambda i, j: (0, nf + cur(j))),
            pl.BlockSpec((None, fb, d), lambda i, j: (layer, prv(j), 0)),
            pl.BlockSpec((tt, b, d), lambda i, j: (i, 0, 0)),
            pl.BlockSpec((None, None, b, d), lambda i, j: (layer, sel(i), 0, 5)),
            _const_spec((1, d)),
        ],
        out_specs=(pl.BlockSpec((b, tt, d), lambda i, j: (0, i, 0)) if batch_major_out
                   else pl.BlockSpec((tt, b, d), lambda i, j: (i, 0, 0))),
        out_shape=jax.ShapeDtypeStruct((b, n_tiles * tt, d) if batch_major_out else xs.shape, F32),
        scratch_shapes=[
            pltpu.VMEM((rows, d), F32),
            pltpu.VMEM((2, rows, fb), BF16),
            pltpu.VMEM((36, b, fb), F32),
        ] + [pltpu.VMEM((GRID_W + 2, b, fb), F32)] * 8,
        input_output_aliases={} if batch_major_out else {9: 0},
        compiler_params=_params("arbitrary", "arbitrary"),
        name="ffn_down",
    )(z, z, z, z, z, z, w_conv9, w_conv9, w_down, xs, mod, g_post)


def _pick(limit, total):
    t = min(limit, total)
    while total % t:
        t -= 1
    return t


def kernel(x, c, ctx, c_ctx, w_ada, b_ada, w_in, w_pool, pool_scale, ssm_a_re, ssm_a_im, ssm_log_dt,
           ssm_b_re, ssm_b_im, ssm_c_re, ssm_c_im, ssm_d, w_glu, w_out, g_pre_mix, g_post_mix,
           g_pre_ffn, g_post_ffn, w_up, w_conv, w_down):
    bsz, seq, d = x.shape
    n_ctx = ctx.shape[1]
    depth = w_ada.shape[0]
    mixw = w_in.shape[-1]
    sw = ssm_d.shape[-1]
    f = w_down.shape[1]
    assert bsz == SUBLANES and seq % GRID_W == 0 and n_ctx % GRID_W == 0
    assert sw % LANES == 0 and (mixw - sw) % sw == 0 and LANES % ssm_b_re.shape[-1] == 0
    s_all = seq + n_ctx
    tt = GRID_W
    n_lat_tiles = seq // tt
    n_ctx_tiles = n_ctx // tt
    n_tiles = n_lat_tiles + n_ctx_tiles
    tt_up = 2 * GRID_W
    assert seq % tt_up == 0 and n_ctx % tt_up == 0
    nb_up = _pick(1024, 2 * f)
    fb = _pick(256, f)
    lat_chunks = _pick(64, seq // SSM_CHUNK)
    ctx_chunks = _pick(64, n_ctx // SSM_CHUNK)
    assert (seq // SSM_CHUNK) % ctx_chunks == 0

    c16 = jnp.concatenate([c, jnp.broadcast_to(c_ctx[None, :], (SUBLANES, d))], axis=0)
    mod_all = _ada_all(c16, w_ada, b_ada, _pick(1024, w_ada.shape[-1]))

    row = lambda v: v.reshape(1, -1).astype(F32)
    n_pool = len(POOL_WINDOWS)
    pw_all = mixw - sw
    pg = pw_all // n_pool
    w_pool_scaled = w_pool.astype(F32) * pool_scale.astype(F32).reshape(depth, n_pool, 1, pg)
    w_out_pool = jnp.einsum('lgio,lgod->lgid', w_pool_scaled,
                            w_out[:, :pw_all].astype(F32).reshape(depth, n_pool, pg, d),
                            precision=lax.Precision.HIGHEST).reshape(depth, pw_all, d)
    w_out = jnp.concatenate([w_out_pool, w_out[:, pw_all:].astype(F32)], axis=1)
    w_in, w_glu, w_out, w_up, w_down = (v.astype(BF16) for v in (w_in, w_glu, w_out, w_up, w_down))
    ssm_params = (ssm_a_re, ssm_a_im, ssm_log_dt, ssm_b_re, ssm_b_im, ssm_c_re, ssm_c_im)
    mats = [jax.vmap(functools.partial(_ssm_matrices, reverse=dr == 1))(*(v[:, dr] for v in ssm_params))
            for dr in range(2)]
    mats = tuple(jnp.stack(m, axis=1) for m in zip(*mats))
    nbk = sw // LANES
    lane0 = (mixw - sw) // LANES
    h0 = jnp.zeros((2, nbk, bsz, mats[0].shape[-1]), F32)
    for l in range(depth):
        last = l == depth - 1
        if l == 0:
            xs, u = _premix_first(x, ctx, mod_all, row(g_pre_mix[l]), w_in, l, tt)
        else:
            u = _premix(xs, mod_all, row(g_pre_mix[l]), w_in, l, tt, n_lat_tiles)
        u5 = u.reshape(s_all // SSM_CHUNK, SSM_CHUNK, bsz, mixw)
        y_ctx, h_ctx = _ssm(u5, mats, l, h0, ctx_chunks, (seq // SSM_CHUNK) // ctx_chunks,
                            (n_ctx // SSM_CHUNK) // ctx_chunks, lane0)
        y_lat, _ = _ssm(u5, mats, l, h_ctx, lat_chunks, 0, (seq // SSM_CHUNK) // lat_chunks, lane0)

        nt = n_lat_tiles if last else n_tiles
        xs, h_ffn = _mix(u, y_lat.reshape(2, seq, bsz, sw), y_ctx.reshape(2, n_ctx, bsz, sw), xs,
                         mod_all, row(ssm_d[l]), w_glu, w_out, row(g_post_mix[l]), row(g_pre_ffn[l]), l,
                         tt, nt, n_lat_tiles, seq, n_ctx)
        nt_up = (seq if last else s_all) // tt_up
        z = _ffn_up(h_ffn, w_up, l, _pick(2048, math.gcd(seq, n_ctx) * bsz), nb_up, fb)
        xs = _ffn_down(z, w_conv[l].reshape(9, 2 * f).astype(F32), w_down, xs, mod_all,
                       row(g_post_ffn[l]), l, fb, nt_up, seq // tt_up, n_ctx // tt_up, last)
    return xs
```

```python
import functools
import math

import jax
import jax.numpy as jnp
from jax import lax
from jax.experimental import pallas as pl
from jax.experimental.pallas import tpu as pltpu

GRID_W = 64
POOL_WINDOWS = (2, 4, 8, 16)
POOL_HALO = max(POOL_WINDOWS) // 2
EPS = 1e-6
SUBLANES = 8
LANES = 128
SSM_CHUNK = 8
VMEM_LIMIT_BYTES = 62 * 1024 * 1024

F32 = jnp.float32
BF16 = jnp.bfloat16


def _params(*sem):
    return pltpu.CompilerParams(dimension_semantics=sem, vmem_limit_bytes=VMEM_LIMIT_BYTES)


def _const_spec(shape, layer=None):
    zeros = (0,) * len(shape)
    if layer is None:
        return pl.BlockSpec(shape, lambda *_: zeros, pipeline_mode=pl.Buffered(1))
    return pl.BlockSpec((None,) + tuple(shape), lambda *_: (layer,) + zeros, pipeline_mode=pl.Buffered(1))


def _rms(v, gain):
    ms = jnp.mean(v * v, axis=-1, keepdims=True)
    return v * lax.rsqrt(ms + EPS) * gain


def _ada_kernel(c_ref, w_ref, b_ref, o_ref):
    s = jax.nn.silu(c_ref[...]).astype(BF16)
    r = jnp.dot(s, w_ref[...].astype(BF16), preferred_element_type=F32) + b_ref[...]
    o_ref[...] = r.reshape(o_ref.shape)


def _ada_all(c16, w_ada, b_ada, nb):
    depth, d, n = w_ada.shape
    return pl.pallas_call(
        _ada_kernel,
        grid=(depth, n // nb),
        in_specs=[
            pl.BlockSpec((2 * SUBLANES, d), lambda l, j: (0, 0)),
            pl.BlockSpec((None, d, nb), lambda l, j: (l, 0, j)),
            pl.BlockSpec((None, 1, nb), lambda l, j: (l, 0, j)),
        ],
        out_specs=pl.BlockSpec((None, 2, SUBLANES, nb), lambda l, j: (l, 0, 0, j)),
        out_shape=jax.ShapeDtypeStruct((depth, 2, SUBLANES, n), F32),
        compiler_params=_params("arbitrary", "arbitrary"),
        name="ada",
    )(c16, w_ada, b_ada.reshape(depth, 1, n))


def _premix_kernel(x_ref, sh_ref, sc_ref, g_ref, w_ref, u_ref):
    tt, b, d = x_ref.shape
    h = _rms(x_ref[...], g_ref[...]) * (1.0 + sc_ref[...]) + sh_ref[...]
    u = jnp.dot(h.reshape(tt * b, d).astype(BF16), w_ref[...], preferred_element_type=F32)
    u_ref[...] = u.reshape(u_ref.shape)


def _premix_first_kernel(xl_ref, xc_ref, sh_ref, sc_ref, g_ref, w_ref, xs_ref, u_ref, *, n_lat_tiles):
    tt = xs_ref.shape[0]
    is_ctx = pl.program_id(0) >= n_lat_tiles

    @pl.when(is_ctx)
    def _():
        for t in range(tt):
            xs_ref[t] = xc_ref[:, t, :]

    @pl.when(jnp.logical_not(is_ctx))
    def _():
        for t in range(tt):
            xs_ref[t] = xl_ref[:, t, :]

    _premix_kernel(xs_ref, sh_ref, sc_ref, g_ref, w_ref, u_ref)


def _premix_first(x, ctx, mod, g, w, layer, tt):
    b, seq, d = x.shape
    n_ctx = ctx.shape[1]
    n = w.shape[-1]
    n_lat_tiles = seq // tt
    sel = lambda i: jnp.where(i >= n_lat_tiles, 1, 0)
    stream = jax.ShapeDtypeStruct((seq + n_ctx, b, d), F32)
    return pl.pallas_call(
        functools.partial(_premix_first_kernel, n_lat_tiles=n_lat_tiles),
        grid=((seq + n_ctx) // tt,),
        in_specs=[
            pl.BlockSpec((b, tt, d), lambda i: (0, jnp.minimum(i, n_lat_tiles - 1), 0)),
            pl.BlockSpec((b, tt, d), lambda i: (0, jnp.maximum(i - n_lat_tiles, 0), 0)),
            pl.BlockSpec((None, None, b, d), lambda i: (layer, sel(i), 0, 0)),
            pl.BlockSpec((None, None, b, d), lambda i: (layer, sel(i), 0, 1)),
            _const_spec((1, d)),
            _const_spec((d, n), layer),
        ],
        out_specs=[pl.BlockSpec((tt, b, d), lambda i: (i, 0, 0)), pl.BlockSpec((tt, b, n), lambda i: (i, 0, 0))],
        out_shape=[stream, jax.ShapeDtypeStruct((seq + n_ctx, b, n), F32)],
        compiler_params=_params("arbitrary"),
        name="premix_first",
    )(x, ctx, mod, mod, g, w)


def _premix(xs, mod, g, w, layer, tt, n_lat_tiles):
    s, b, d = xs.shape
    n = w.shape[-1]
    sel = lambda i: jnp.where(i >= n_lat_tiles, 1, 0)
    return pl.pallas_call(
        _premix_kernel,
        grid=(s // tt,),
        in_specs=[
            pl.BlockSpec((tt, b, d), lambda i: (i, 0, 0)),
            pl.BlockSpec((None, None, b, d), lambda i: (layer, sel(i), 0, 0)),
            pl.BlockSpec((None, None, b, d), lambda i: (layer, sel(i), 0, 1)),
            _const_spec((1, d)),
            _const_spec((d, n), layer),
        ],
        out_specs=pl.BlockSpec((tt, b, n), lambda i: (i, 0, 0)),
        out_shape=jax.ShapeDtypeStruct((s, b, n), F32),
        compiler_params=_params("arbitrary"),
        name="premix",
    )(xs, mod, mod, g, w)


def _ssm_kernel(u_ref, mo_ref, my_ref, lam_ref, h0_ref, y_ref, hf_ref, s_ref, hin_ref, hc_ref, *,
                chunks):
    c_n = chunks
    sb = hc_ref.shape[-1]
    half = sb // 2
    d = pl.program_id(0)
    t = pl.program_id(2)

    @pl.when(t == 0)
    def _():
        hc_ref[...] = h0_ref[...]

    u_rows = jnp.concatenate(
        [u_ref[:, k].reshape(c_n * SUBLANES, LANES).astype(BF16) for k in range(SSM_CHUNK)], axis=-1)
    s_ref[...] = jnp.dot(u_rows, mo_ref[...], preferred_element_type=F32).reshape(s_ref.shape)

    lre = jnp.broadcast_to(lam_ref[0:1, :], (SUBLANES, half))
    lim = jnp.broadcast_to(lam_ref[1:2, :], (SUBLANES, half))

    def step(i, carry):
        hr, hi = carry
        c = jnp.where(d == 0, i, c_n - 1 - i)
        hin_ref[c, :, :half] = hr
        hin_ref[c, :, half:] = hi
        sr = s_ref[c, :, :half]
        si = s_ref[c, :, half:]
        return lre * hr - lim * hi + sr, lre * hi + lim * hr + si

    hr, hi = lax.fori_loop(0, c_n, step, (hc_ref[:, :half], hc_ref[:, half:]))
    hc_ref[:, :half] = hr
    hc_ref[:, half:] = hi
    hf_ref[:, :half] = hr
    hf_ref[:, half:] = hi

    h_rows = hin_ref[...].reshape(c_n * SUBLANES, sb).astype(BF16)
    y = jnp.dot(jnp.concatenate([u_rows, h_rows], axis=-1), my_ref[...], preferred_element_type=F32)
    for k in range(SSM_CHUNK):
        y_ref[:, k] = y[:, k * LANES:(k + 1) * LANES].reshape(c_n, SUBLANES, LANES)


def _ssm(u5, mats, layer, h0, chunks, block_off, n_tiles, ssm_lane_block0):
    m_out, m_y, lam8 = mats
    b = u5.shape[2]
    _, _, nb, kk, sb = m_out.shape
    tile = lambda d, t: jnp.where(d == 0, t, n_tiles - 1 - t)
    y_shape = (2, n_tiles * chunks, SSM_CHUNK, b, nb * LANES)
    return pl.pallas_call(
        functools.partial(_ssm_kernel, chunks=chunks),
        grid=(2, nb, n_tiles),
        in_specs=[
            pl.BlockSpec((chunks, SSM_CHUNK, b, LANES),
                         lambda d, j, t: (tile(d, t) + block_off, 0, 0, ssm_lane_block0 + j)),
            pl.BlockSpec((None, None, None, kk, sb), lambda d, j, t: (layer, d, j, 0, 0)),
            pl.BlockSpec((None, None, None, kk + sb, kk), lambda d, j, t: (layer, d, j, 0, 0)),
            pl.BlockSpec((None, None, None, 2, sb // 2), lambda d, j, t: (layer, d, j, 0, 0)),
            pl.BlockSpec((None, None, b, sb), lambda d, j, t: (d, j, 0, 0)),
        ],
        out_specs=[
            pl.BlockSpec((None, chunks, SSM_CHUNK, b, LANES), lambda d, j, t: (d, tile(d, t), 0, 0, j)),
            pl.BlockSpec((None, None, b, sb), lambda d, j, t: (d, j, 0, 0)),
        ],
        out_shape=[jax.ShapeDtypeStruct(y_shape, F32), jax.ShapeDtypeStruct(h0.shape, F32)],
        scratch_shapes=[
            pltpu.VMEM((chunks, b, sb), F32),
            pltpu.VMEM((chunks, b, sb), F32),
            pltpu.VMEM((b, sb), F32),
        ],
        compiler_params=_params("arbitrary", "arbitrary", "arbitrary"),
        name="ssm",
    )(u5, m_out, m_y, lam8, h0)


def _ssm_matrices(a_re, a_im, log_dt, b_re, b_im, c_re, c_im, reverse):
    g, p = a_re.shape
    h = b_re.shape[-1]
    t_n = SSM_CHUNK
    gpb = LANES // h
    nb = g // gpb
    a_re = a_re.astype(F32)
    a_im = a_im.astype(F32)
    dt = jnp.exp(log_dt.astype(F32))[:, None]
    mag = jnp.exp(a_re * dt)
    lam_re = mag * jnp.cos(a_im * dt)
    lam_im = mag * jnp.sin(a_im * dt)
    denom = a_re * a_re + a_im * a_im
    nr, ni = lam_re - 1.0, lam_im
    f_re = ((nr * a_re + ni * a_im) / denom)[..., None]
    f_im = ((ni * a_re - nr * a_im) / denom)[..., None]
    b_re = b_re.astype(F32)
    b_im = b_im.astype(F32)
    bb_re = f_re * b_re - f_im * b_im
    bb_im = f_re * b_im + f_im * b_re
    c_re = c_re.astype(F32)
    c_im = c_im.astype(F32)
    pr, pi = [jnp.ones_like(lam_re)], [jnp.zeros_like(lam_im)]
    for _ in range(t_n):
        pr, pi = (pr + [pr[-1] * lam_re - pi[-1] * lam_im], pi + [pr[-1] * lam_im + pi[-1] * lam_re])
    pr = jnp.stack(pr)
    pi = jnp.stack(pi)

    q_re = pr[:t_n, :, :, None] * bb_re - pi[:t_n, :, :, None] * bb_im
    q_im = pr[:t_n, :, :, None] * bb_im + pi[:t_n, :, :, None] * bb_re
    hi = lax.Precision.HIGHEST
    taps = (jnp.einsum('gop,kgpi->kgoi', c_re, q_re, precision=hi)
            - jnp.einsum('gop,kgpi->kgoi', c_im, q_im, precision=hi))
    lag = jnp.arange(t_n)[None, :] - jnp.arange(t_n)[:, None]
    toep = jnp.where((lag >= 0)[:, :, None, None, None], taps[jnp.clip(lag, 0, t_n - 1)], 0.0)
    toep = toep.reshape(t_n, t_n, nb, gpb, h, h)
    cr, ci = c_re[None], c_im[None]
    lr, li = pr[1:, :, None, :], pi[1:, :, None, :]
    w_in = jnp.stack([cr * lr - ci * li, -(cr * li + ci * lr)])
    w_in = w_in.reshape(2, t_n, nb, gpb, h, p)
    w_out = jnp.stack([q_re[::-1], q_im[::-1]]).reshape(2, t_n, nb, gpb, p, h)
    if reverse:
        toep = toep[::-1, ::-1]
        w_in = w_in[:, ::-1]
        w_out = w_out[:, ::-1]
    k_dim = t_n * gpb * h
    sb = 2 * gpb * p

    def on_group_diagonal(compact, row_width, col_width, n_cols):
        n_rows, n_compact = compact.shape[1:]
        col = jnp.arange(n_cols)
        src = (col // (gpb * col_width)) * col_width + col % col_width
        place = (jnp.arange(n_compact)[:, None] == src[None, :]).astype(F32)
        full = jnp.einsum('jrk,kc->jrc', compact, place)
        row_group = (jnp.arange(n_rows) // row_width) % gpb
        col_group = (col // col_width) % gpb
        return jnp.where(row_group[:, None] == col_group[None, :], full, 0.0).astype(BF16)

    m_toep = on_group_diagonal(toep.transpose(2, 0, 3, 5, 1, 4).reshape(nb, k_dim, t_n * h), h, h, k_dim)
    m_in = on_group_diagonal(w_in.transpose(2, 0, 3, 5, 1, 4).reshape(nb, sb, t_n * h), p, h, k_dim)
    m_out = on_group_diagonal(w_out.transpose(2, 1, 3, 5, 0, 4).reshape(nb, k_dim, 2 * p), h, p, sb)
    m_y = jnp.concatenate([m_toep, m_in], axis=1)
    lam8 = jnp.stack([pr[t_n].reshape(nb, gpb * p), pi[t_n].reshape(nb, gpb * p)], axis=1)
    return m_out, m_y, lam8


def _mix_kernel(um_ref, up_ref, un_ref, us_ref, yl_ref, yc_ref, x_ref, gate_ref, sh_ref, sc_ref,
                sd_ref, wg_ref, wo_ref, gp_ref, gf_ref, o_ref, h_ref, ext_ref, cat_ref, *,
                n_lat_tiles, seq, ctx):
    tt, b, pw_all = um_ref.shape
    pg = pw_all // len(POOL_WINDOWS)
    sw = us_ref.shape[-1]
    rows = tt * b
    i = pl.program_id(0)
    is_ctx = i >= n_lat_tiles
    t0 = jnp.where(is_ctx, i - n_lat_tiles, i) * tt
    seg = jnp.where(is_ctx, ctx, seq)
    has_prev = (t0 > 0).astype(F32)
    has_next = (t0 + tt < seg).astype(F32)
    hl = POOL_HALO
    ext_ref[0:hl] = up_ref[...] * has_prev
    ext_ref[hl:hl + tt] = um_ref[...]
    ext_ref[hl + tt:hl + tt + hl] = un_ref[...] * has_next

    tl = t0 + lax.broadcasted_iota(jnp.int32, (tt, b, pg), 0)
    for g, w in enumerate(POOL_WINDOWS):
        sl = slice(g * pg, (g + 1) * pg)
        lo = hl - w // 2
        acc = ext_ref[lo:lo + tt, :, sl]
        for k in range(1, w):
            acc = acc + ext_ref[lo + k:lo + k + tt, :, sl]
        cnt = (jnp.minimum(tl + w // 2, seg) - jnp.maximum(tl - w // 2, 0)).astype(F32)
        p = acc / cnt - ext_ref[hl:hl + tt, :, sl]
        cat_ref[:, sl] = p.reshape(rows, pg).astype(BF16)

    y = jnp.where(is_ctx, yc_ref[0] + yc_ref[1], yl_ref[0] + yl_ref[1])
    yv = y.reshape(rows, sw) + sd_ref[...] * us_ref[...].reshape(rows, sw)
    yv = jax.nn.gelu(yv)
    glu = jax.nn.sigmoid(jnp.dot(yv.astype(BF16), wg_ref[...], preferred_element_type=F32))
    cat_ref[:, pw_all:] = (yv * glu).astype(BF16)

    mix = jnp.dot(cat_ref[...], wo_ref[...], preferred_element_type=F32)
    x_new = x_ref[...] + gate_ref[...] * _rms(mix, gp_ref[...]).reshape(o_ref.shape)
    o_ref[...] = x_new
    h = _rms(x_new, gf_ref[...]) * (1.0 + sc_ref[...]) + sh_ref[...]
    h_ref[...] = h.reshape(h_ref.shape).astype(BF16)


def _mix(u, y_lat, y_ctx, xs, mod, ssm_d, w_glu, w_out, g_post, g_ffn, layer, tt, n_tiles,
         n_lat_tiles, seq, ctx):
    s, b, d = xs.shape
    mixw = u.shape[-1]
    sw = y_lat.shape[-1]
    pw_all = mixw - sw
    hl = POOL_HALO
    hpt = tt // hl
    last_halo = n_tiles * hpt - 1
    sel = lambda i: jnp.where(i >= n_lat_tiles, 1, 0)
    return pl.pallas_call(
        functools.partial(_mix_kernel, n_lat_tiles=n_lat_tiles, seq=seq, ctx=ctx),
        grid=(n_tiles,),
        in_specs=[
            pl.BlockSpec((tt, b, pw_all), lambda i: (i, 0, 0)),
            pl.BlockSpec((hl, b, pw_all), lambda i: (jnp.maximum(i * hpt - 1, 0), 0, 0)),
            pl.BlockSpec((hl, b, pw_all), lambda i: (jnp.minimum((i + 1) * hpt, last_halo), 0, 0)),
            pl.BlockSpec((tt, b, sw), lambda i: (i, 0, pw_all // sw)),
            pl.BlockSpec((2, tt, b, sw), lambda i: (0, jnp.minimum(i, n_lat_tiles - 1), 0, 0)),
            pl.BlockSpec((2, tt, b, sw), lambda i: (0, jnp.maximum(i - n_lat_tiles, 0), 0, 0)),
            pl.BlockSpec((tt, b, d), lambda i: (i, 0, 0)),
            pl.BlockSpec((None, None, b, d), lambda i: (layer, sel(i), 0, 2)),
            pl.BlockSpec((None, None, b, d), lambda i: (layer, sel(i), 0, 3)),
            pl.BlockSpec((None, None, b, d), lambda i: (layer, sel(i), 0, 4)),
            _const_spec((1, sw)),
            _const_spec(w_glu.shape[1:], layer),
            _const_spec(w_out.shape[1:], layer),
            _const_spec((1, d)),
            _const_spec((1, d)),
        ],
        out_specs=[pl.BlockSpec((tt, b, d), lambda i: (i, 0, 0)),
                   pl.BlockSpec((tt * b, d), lambda i: (i, 0))],
        out_shape=[jax.ShapeDtypeStruct(xs.shape, F32),
                   jax.ShapeDtypeStruct((n_tiles * tt * b, d), BF16)],
        scratch_shapes=[
            pltpu.VMEM((tt + 2 * hl, b, pw_all), F32),
            pltpu.VMEM((tt * b, mixw), BF16),
        ],
        input_output_aliases={6: 0},
        compiler_params=_params("arbitrary"),
        name="mix",
    )(u, u, u, u, y_lat, y_ctx, xs, mod, mod, mod, ssm_d, w_glu, w_out, g_post, g_ffn)


def _ffn_up_kernel(h_ref, w_ref, z_ref):
    z = jnp.dot(h_ref[...], w_ref[...], preferred_element_type=F32).astype(BF16)
    fb = z_ref.shape[-1]
    for k in range(z_ref.shape[0]):
        z_ref[k] = z[:, k * fb:(k + 1) * fb]


def _ffn_up(h, w_up, layer, tm, nb, fb):
    rows, d = h.shape
    n = w_up.shape[-1]
    return pl.pallas_call(
        _ffn_up_kernel,
        grid=(rows // tm, n // nb),
        in_specs=[
            pl.BlockSpec((tm, d), lambda i, j: (i, 0)),
            pl.BlockSpec((None, d, nb), lambda i, j: (layer, 0, j)),
        ],
        out_specs=pl.BlockSpec((nb // fb, tm, fb), lambda i, j: (j, i, 0)),
        out_shape=jax.ShapeDtypeStruct((n // fb, rows, fb), BF16),
        compiler_params=_params("arbitrary", "arbitrary"),
        name="ffn_up",
    )(h, w_up)


FFN_QUARTERS = 4
CONV_PIECE_STEPS = 4


def _ffn_down_kernel(va_ref, vm_ref, vb_ref, ga_ref, gm_ref, gb_ref, kv_ref, kg_ref, wd_ref, x_ref,
                     gate_ref, gp_ref, o_ref, *scratch, n_lat_tiles, n_ctx_tiles, batch_major_out):
    acc_ref = scratch[0] if batch_major_out else None
    (act_ref, kb_ref, eva_ref, evm0_ref, evm1_ref, evb_ref, ega_ref, egm0_ref, egm1_ref,
     egb_ref) = scratch[1:] if batch_major_out else scratch
    w, b, fb = evm0_ref.shape[0] - 2, evm0_ref.shape[1], evm0_ref.shape[2]
    i = pl.program_id(0)
    j = pl.program_id(1)
    nf = pl.num_programs(1) - 1
    is_ctx = i >= n_lat_tiles
    is_lat = jnp.logical_not(is_ctx)
    ic = i - n_lat_tiles
    lat_f = is_lat.astype(F32)
    ctx_f = is_ctx.astype(F32)
    vert_up = jnp.logical_and(is_lat, i > 0).astype(F32)
    vert_dn = jnp.logical_and(is_lat, i < n_lat_tiles - 1).astype(F32)
    hz_prev = jnp.logical_and(is_ctx, ic > 0).astype(F32)
    hz_next = jnp.logical_and(is_ctx, ic < n_ctx_tiles - 1).astype(F32)
    slot = j % 2

    @pl.when(j == 0)
    def _():
        if batch_major_out:
            acc_ref[...] = jnp.zeros(acc_ref.shape, F32)
        else:
            o_ref[...] = jnp.zeros(o_ref.shape, F32)
        act_ref[1] = jnp.zeros(act_ref.shape[1:], BF16)

    streams = ((va_ref, vm_ref, vb_ref, (eva_ref, evm0_ref, evm1_ref, evb_ref), kv_ref),
               (ga_ref, gm_ref, gb_ref, (ega_ref, egm0_ref, egm1_ref, egb_ref), kg_ref))
    pair = 2 * b
    zero_row = jnp.zeros((1, b, fb), F32)
    edge = lambda ref, row0, part: ref[row0:row0 + pair, :].astype(F32)[part * b:(part + 1) * b]
    for which, (a_ref, m_ref, b_ref, (ea_ref, em0_ref, em1_ref, eb_ref), k_ref) in enumerate(streams):
        ea_ref[0:1] = zero_row
        ea_ref[w + 1:w + 2] = zero_row
        eb_ref[0:1] = zero_row
        eb_ref[w + 1:w + 2] = zero_row
        em0_ref[0:1] = (edge(a_ref, (w - 2) * b, 1) * hz_prev).reshape(1, b, fb)
        em0_ref[w + 1:w + 2] = (edge(m_ref, w * b, 0) * ctx_f).reshape(1, b, fb)
        em1_ref[0:1] = (edge(m_ref, (w - 2) * b, 1) * ctx_f).reshape(1, b, fb)
        em1_ref[w + 1:w + 2] = (edge(b_ref, 0, 0) * hz_next).reshape(1, b, fb)
        for row, flags in enumerate(((vert_up, None, lat_f), (lat_f, None, vert_dn))):
            for r, flag in enumerate(flags):
                for dc in range(3):
                    kr = k_ref[3 * r + dc:3 * r + dc + 1, :]
                    kr = kr if flag is None else kr * flag
                    kb_ref[18 * which + 9 * row + 3 * r + dc] = jnp.broadcast_to(kr, (b, fb))

    tq = w // FFN_QUARTERS
    nq_lanes = x_ref.shape[-1] // FFN_QUARTERS
    tp = CONV_PIECE_STEPS

    def stage(q):
        lo, hi = q * tq * b, (q + 1) * tq * b
        for a_ref, m_ref, b_ref, (ea_ref, em0_ref, em1_ref, eb_ref), _ in streams:
            for src, off, dst in ((a_ref, 0, ea_ref), (m_ref, 0, em0_ref), (m_ref, w * b, em1_ref),
                                  (b_ref, 0, eb_ref)):
                blk = src[off + lo:off + hi, :].astype(F32)
                dst[1 + q * tq:1 + (q + 1) * tq] = blk.reshape(tq, b, fb)

    def conv_piece(row, t0, lanes, which):
        out = None
        for r, e_ref in enumerate(streams[which][3][row:row + 3]):
            for dc in range(3):
                tap = kb_ref[18 * which + 9 * row + 3 * r + dc, :, lanes]
                term = e_ref[t0 + dc:t0 + dc + tp, :, lanes] * tap
                out = term if out is None else out + term
        return out

    stage(0)
    for q in range(FFN_QUARTERS):
        if q + 1 < FFN_QUARTERS:
            stage(q + 1)
        cols = slice(q * nq_lanes, (q + 1) * nq_lanes)
        part = jnp.dot(act_ref[1 - slot], wd_ref[:, cols], preferred_element_type=F32)
        if batch_major_out:
            acc_ref[:, cols] += part
        else:
            o_ref[:, :, cols] += part.reshape(2 * w, b, nq_lanes)
        for row in range(2):
            for lc in range(fb // LANES):
                lanes = slice(lc * LANES, (lc + 1) * LANES)
                for t0 in range(q * tq, (q + 1) * tq, tp):
                    val = conv_piece(row, t0, lanes, 0)
                    gat = conv_piece(row, t0, lanes, 1)
                    act = (val * jax.nn.silu(gat)).reshape(tp * b, LANES).astype(BF16)
                    act_ref[slot, (row * w + t0) * b:(row * w + t0 + tp) * b, lanes] = act

    @pl.when(j == nf)
    def _():
        if batch_major_out:
            out = x_ref[...] + gate_ref[...] * _rms(acc_ref[...], gp_ref[...]).reshape(x_ref.shape)
            for t in range(out.shape[0]):
                o_ref[:, t, :] = out[t]
        else:
            o_ref[...] = x_ref[...] + gate_ref[...] * _rms(o_ref[...], gp_ref[...])


def _ffn_down(z, w_conv9, w_down, xs, mod, g_post, layer, fb, n_tiles, n_lat_tiles, n_ctx_tiles,
              batch_major_out):
    s, b, d = xs.shape
    f = w_down.shape[1]
    nf = f // fb
    tt = 2 * GRID_W
    rows = tt * b
    half = rows // 2
    up = lambda i: jnp.maximum(2 * i - 1, 0)
    dn = lambda i: jnp.minimum(2 * i + 2, 2 * n_tiles - 1)
    sel = lambda i: jnp.where(i >= n_lat_tiles, 1, 0)
    cur = lambda j: jnp.minimum(j, nf - 1)
    prv = lambda j: jnp.maximum(j - 1, 0)
    zspec = lambda n_rows, row, col0: pl.BlockSpec((None, n_rows, fb),
                                                   lambda i, j: (col0 + cur(j), row(i), 0))
    same = lambda i: i
    assert GRID_W % (FFN_QUARTERS * SUBLANES) == 0 and d % (FFN_QUARTERS * LANES) == 0
    return pl.pallas_call(
        functools.partial(_ffn_down_kernel, n_lat_tiles=n_lat_tiles, n_ctx_tiles=n_ctx_tiles,
                          batch_major_out=batch_major_out),
        grid=(n_tiles, nf + 1),
        in_specs=[
            zspec(half, up, 0), zspec(rows, same, 0), zspec(half, dn, 0),
            zspec(half, up, nf), zspec(rows, same, nf), zspec(half, dn, nf),
            pl.BlockSpec((9, fb), lambda i, j: (0, cur(j))),
            pl.BlockSpec((9, fb), lambda i, j: (0, nf + cur(j))),
            pl.BlockSpec((None, fb, d), lambda i, j: (layer, prv(j), 0)),
            pl.BlockSpec((tt, b, d), lambda i, j: (i, 0, 0)),
            pl.BlockSpec((None, None, b, d), lambda i, j: (layer, sel(i), 0, 5)),
            _const_spec((1, d)),
        ],
        out_specs=(pl.BlockSpec((b, tt, d), lambda i, j: (0, i, 0)) if batch_major_out
                   else pl.BlockSpec((tt, b, d), lambda i, j: (i, 0, 0))),
        out_shape=jax.ShapeDtypeStruct((b, n_tiles * tt, d) if batch_major_out else xs.shape, F32),
        scratch_shapes=([pltpu.VMEM((rows, d), F32)] if batch_major_out else []) + [
            pltpu.VMEM((2, rows, fb), BF16),
            pltpu.VMEM((36, b, fb), F32),
        ] + [pltpu.VMEM((GRID_W + 2, b, fb), F32)] * 8,
        input_output_aliases={} if batch_major_out else {9: 0},
        compiler_params=_params("arbitrary", "arbitrary"),
        name="ffn_down",
    )(z, z, z, z, z, z, w_conv9, w_conv9, w_down, xs, mod, g_post)


def _pick(limit, total):
    t = min(limit, total)
    while total % t:
        t -= 1
    return t


def kernel(x, c, ctx, c_ctx, w_ada, b_ada, w_in, w_pool, pool_scale, ssm_a_re, ssm_a_im, ssm_log_dt,
           ssm_b_re, ssm_b_im, ssm_c_re, ssm_c_im, ssm_d, w_glu, w_out, g_pre_mix, g_post_mix,
           g_pre_ffn, g_post_ffn, w_up, w_conv, w_down):
    bsz, seq, d = x.shape
    n_ctx = ctx.shape[1]
    depth = w_ada.shape[0]
    mixw = w_in.shape[-1]
    sw = ssm_d.shape[-1]
    f = w_down.shape[1]
    assert bsz == SUBLANES and seq % GRID_W == 0 and n_ctx % GRID_W == 0
    assert sw % LANES == 0 and (mixw - sw) % sw == 0 and LANES % ssm_b_re.shape[-1] == 0
    s_all = seq + n_ctx
    tt = GRID_W
    n_lat_tiles = seq // tt
    n_ctx_tiles = n_ctx // tt
    n_tiles = n_lat_tiles + n_ctx_tiles
    tt_up = 2 * GRID_W
    assert seq % tt_up == 0 and n_ctx % tt_up == 0
    nb_up = _pick(1024, 2 * f)
    fb_in_place = _pick(512, f)
    fb_fresh = _pick(256, f)
    lat_chunks = _pick(64, seq // SSM_CHUNK)
    ctx_chunks = _pick(64, n_ctx // SSM_CHUNK)
    assert (seq // SSM_CHUNK) % ctx_chunks == 0

    c16 = jnp.concatenate([c, jnp.broadcast_to(c_ctx[None, :], (SUBLANES, d))], axis=0)
    mod_all = _ada_all(c16, w_ada, b_ada, _pick(1024, w_ada.shape[-1]))

    row = lambda v: v.reshape(1, -1).astype(F32)
    n_pool = len(POOL_WINDOWS)
    pw_all = mixw - sw
    pg = pw_all // n_pool
    w_pool_scaled = w_pool.astype(F32) * pool_scale.astype(F32).reshape(depth, n_pool, 1, pg)
    w_out_pool = jnp.einsum('lgio,lgod->lgid', w_pool_scaled,
                            w_out[:, :pw_all].astype(F32).reshape(depth, n_pool, pg, d),
                            precision=lax.Precision.HIGHEST).reshape(depth, pw_all, d)
    w_out = jnp.concatenate([w_out_pool, w_out[:, pw_all:].astype(F32)], axis=1)
    w_in, w_glu, w_out, w_up, w_down = (v.astype(BF16) for v in (w_in, w_glu, w_out, w_up, w_down))
    ssm_params = (ssm_a_re, ssm_a_im, ssm_log_dt, ssm_b_re, ssm_b_im, ssm_c_re, ssm_c_im)
    mats = [jax.vmap(functools.partial(_ssm_matrices, reverse=dr == 1))(*(v[:, dr] for v in ssm_params))
            for dr in range(2)]
    mats = tuple(jnp.stack(m, axis=1) for m in zip(*mats))
    nbk = sw // LANES
    lane0 = (mixw - sw) // LANES
    h0 = jnp.zeros((2, nbk, bsz, mats[0].shape[-1]), F32)
    for l in range(depth):
        last = l == depth - 1
        if l == 0:
            xs, u = _premix_first(x, ctx, mod_all, row(g_pre_mix[l]), w_in, l, tt)
        else:
            u = _premix(xs, mod_all, row(g_pre_mix[l]), w_in, l, tt, n_lat_tiles)
        u5 = u.reshape(s_all // SSM_CHUNK, SSM_CHUNK, bsz, mixw)
        y_ctx, h_ctx = _ssm(u5, mats, l, h0, ctx_chunks, (seq // SSM_CHUNK) // ctx_chunks,
                            (n_ctx // SSM_CHUNK) // ctx_chunks, lane0)
        y_lat, _ = _ssm(u5, mats, l, h_ctx, lat_chunks, 0, (seq // SSM_CHUNK) // lat_chunks, lane0)

        nt = n_lat_tiles if last else n_tiles
        xs, h_ffn = _mix(u, y_lat.reshape(2, seq, bsz, sw), y_ctx.reshape(2, n_ctx, bsz, sw), xs,
                         mod_all, row(ssm_d[l]), w_glu, w_out, row(g_post_mix[l]), row(g_pre_ffn[l]), l,
                         tt, nt, n_lat_tiles, seq, n_ctx)
        nt_up = (seq if last else s_all) // tt_up
        fb = fb_fresh if last else fb_in_place
        z = _ffn_up(h_ffn, w_up, l, _pick(2048, math.gcd(seq, n_ctx) * bsz), nb_up, fb)
        xs = _ffn_down(z, w_conv[l].reshape(9, 2 * f).astype(F32), w_down, xs, mod_all,
                       row(g_post_ffn[l]), l, fb, nt_up, seq // tt_up, n_ctx // tt_up, last)
    return xs
```

```python
import functools
import math

import jax
import jax.numpy as jnp
from jax import lax
from jax.experimental import pallas as pl
from jax.experimental.pallas import tpu as pltpu

GRID_W = 64
POOL_WINDOWS = (2, 4, 8, 16)
POOL_HALO = max(POOL_WINDOWS) // 2
EPS = 1e-6
SUBLANES = 8
LANES = 128
SSM_CHUNK = 8
VMEM_LIMIT_BYTES = 62 * 1024 * 1024

F32 = jnp.float32
BF16 = jnp.bfloat16


def _params(*sem):
    return pltpu.CompilerParams(dimension_semantics=sem, vmem_limit_bytes=VMEM_LIMIT_BYTES)


def _const_spec(shape, layer=None):
    zeros = (0,) * len(shape)
    if layer is None:
        return pl.BlockSpec(shape, lambda *_: zeros, pipeline_mode=pl.Buffered(1))
    return pl.BlockSpec((None,) + tuple(shape), lambda *_: (layer,) + zeros, pipeline_mode=pl.Buffered(1))


def _rms(v, gain):
    ms = jnp.mean(v * v, axis=-1, keepdims=True)
    return v * lax.rsqrt(ms + EPS) * gain


def _ada_kernel(c_ref, w_ref, b_ref, o_ref):
    s = jax.nn.silu(c_ref[...]).astype(BF16)
    r = jnp.dot(s, w_ref[...].astype(BF16), preferred_element_type=F32) + b_ref[...]
    o_ref[...] = r.reshape(o_ref.shape)


def _ada_all(c16, w_ada, b_ada, nb):
    depth, d, n = w_ada.shape
    return pl.pallas_call(
        _ada_kernel,
        grid=(depth, n // nb),
        in_specs=[
            pl.BlockSpec((2 * SUBLANES, d), lambda l, j: (0, 0)),
            pl.BlockSpec((None, d, nb), lambda l, j: (l, 0, j)),
            pl.BlockSpec((None, 1, nb), lambda l, j: (l, 0, j)),
        ],
        out_specs=pl.BlockSpec((None, 2, SUBLANES, nb), lambda l, j: (l, 0, 0, j)),
        out_shape=jax.ShapeDtypeStruct((depth, 2, SUBLANES, n), F32),
        compiler_params=_params("arbitrary", "arbitrary"),
        name="ada",
    )(c16, w_ada, b_ada.reshape(depth, 1, n))


def _premix_kernel(x_ref, sh_ref, sc_ref, g_ref, w_ref, u_ref):
    tt, b, d = x_ref.shape
    h = _rms(x_ref[...], g_ref[...]) * (1.0 + sc_ref[...]) + sh_ref[...]
    u = jnp.dot(h.reshape(tt * b, d).astype(BF16), w_ref[...], preferred_element_type=F32)
    u_ref[...] = u.reshape(u_ref.shape)


def _premix_first_kernel(xl_ref, xc_ref, sh_ref, sc_ref, g_ref, w_ref, xs_ref, u_ref, *, n_lat_tiles):
    tt = xs_ref.shape[0]
    is_ctx = pl.program_id(0) >= n_lat_tiles

    @pl.when(is_ctx)
    def _():
        for t in range(tt):
            xs_ref[t] = xc_ref[:, t, :]

    @pl.when(jnp.logical_not(is_ctx))
    def _():
        for t in range(tt):
            xs_ref[t] = xl_ref[:, t, :]

    _premix_kernel(xs_ref, sh_ref, sc_ref, g_ref, w_ref, u_ref)


def _premix_first(x, ctx, mod, g, w, layer, tt):
    b, seq, d = x.shape
    n_ctx = ctx.shape[1]
    n = w.shape[-1]
    n_lat_tiles = seq // tt
    sel = lambda i: jnp.where(i >= n_lat_tiles, 1, 0)
    stream = jax.ShapeDtypeStruct((seq + n_ctx, b, d), F32)
    return pl.pallas_call(
        functools.partial(_premix_first_kernel, n_lat_tiles=n_lat_tiles),
        grid=((seq + n_ctx) // tt,),
        in_specs=[
            pl.BlockSpec((b, tt, d), lambda i: (0, jnp.minimum(i, n_lat_tiles - 1), 0)),
            pl.BlockSpec((b, tt, d), lambda i: (0, jnp.maximum(i - n_lat_tiles, 0), 0)),
            pl.BlockSpec((None, None, b, d), lambda i: (layer, sel(i), 0, 0)),
            pl.BlockSpec((None, None, b, d), lambda i: (layer, sel(i), 0, 1)),
            _const_spec((1, d)),
            _const_spec((d, n), layer),
        ],
        out_specs=[pl.BlockSpec((tt, b, d), lambda i: (i, 0, 0)), pl.BlockSpec((tt, b, n), lambda i: (i, 0, 0))],
        out_shape=[stream, jax.ShapeDtypeStruct((seq + n_ctx, b, n), F32)],
        compiler_params=_params("arbitrary"),
        name="premix_first",
    )(x, ctx, mod, mod, g, w)


def _premix(xs, mod, g, w, layer, tt, n_lat_tiles):
    s, b, d = xs.shape
    n = w.shape[-1]
    sel = lambda i: jnp.where(i >= n_lat_tiles, 1, 0)
    return pl.pallas_call(
        _premix_kernel,
        grid=(s // tt,),
        in_specs=[
            pl.BlockSpec((tt, b, d), lambda i: (i, 0, 0)),
            pl.BlockSpec((None, None, b, d), lambda i: (layer, sel(i), 0, 0)),
            pl.BlockSpec((None, None, b, d), lambda i: (layer, sel(i), 0, 1)),
            _const_spec((1, d)),
            _const_spec((d, n), layer),
        ],
        out_specs=pl.BlockSpec((tt, b, n), lambda i: (i, 0, 0)),
        out_shape=jax.ShapeDtypeStruct((s, b, n), F32),
        compiler_params=_params("arbitrary"),
        name="premix",
    )(xs, mod, mod, g, w)


def _ssm_kernel(u_ref, mo_ref, my_ref, lam_ref, h0_ref, y_ref, hf_ref, s_ref, hin_ref, hc_ref, *,
                chunks):
    c_n = chunks
    sb = hc_ref.shape[-1]
    half = sb // 2
    d = pl.program_id(0)
    t = pl.program_id(2)

    @pl.when(t == 0)
    def _():
        hc_ref[...] = h0_ref[...]

    u_rows = jnp.concatenate(
        [u_ref[:, k].reshape(c_n * SUBLANES, LANES).astype(BF16) for k in range(SSM_CHUNK)], axis=-1)
    s_ref[...] = jnp.dot(u_rows, mo_ref[...], preferred_element_type=F32).reshape(s_ref.shape)

    lre = jnp.broadcast_to(lam_ref[0:1, :], (SUBLANES, half))
    lim = jnp.broadcast_to(lam_ref[1:2, :], (SUBLANES, half))

    def step(i, carry):
        hr, hi = carry
        c = jnp.where(d == 0, i, c_n - 1 - i)
        hin_ref[c, :, :half] = hr
        hin_ref[c, :, half:] = hi
        sr = s_ref[c, :, :half]
        si = s_ref[c, :, half:]
        return lre * hr - lim * hi + sr, lre * hi + lim * hr + si

    hr, hi = lax.fori_loop(0, c_n, step, (hc_ref[:, :half], hc_ref[:, half:]))
    hc_ref[:, :half] = hr
    hc_ref[:, half:] = hi
    hf_ref[:, :half] = hr
    hf_ref[:, half:] = hi

    h_rows = hin_ref[...].reshape(c_n * SUBLANES, sb).astype(BF16)
    y = jnp.dot(jnp.concatenate([u_rows, h_rows], axis=-1), my_ref[...], preferred_element_type=F32)
    for k in range(SSM_CHUNK):
        y_ref[:, k] = y[:, k * LANES:(k + 1) * LANES].reshape(c_n, SUBLANES, LANES)


def _ssm(u5, mats, layer, h0, chunks, block_off, n_tiles, ssm_lane_block0):
    m_out, m_y, lam8 = mats
    b = u5.shape[2]
    _, _, nb, kk, sb = m_out.shape
    tile = lambda d, t: jnp.where(d == 0, t, n_tiles - 1 - t)
    y_shape = (2, n_tiles * chunks, SSM_CHUNK, b, nb * LANES)
    return pl.pallas_call(
        functools.partial(_ssm_kernel, chunks=chunks),
        grid=(2, nb, n_tiles),
        in_specs=[
            pl.BlockSpec((chunks, SSM_CHUNK, b, LANES),
                         lambda d, j, t: (tile(d, t) + block_off, 0, 0, ssm_lane_block0 + j)),
            pl.BlockSpec((None, None, None, kk, sb), lambda d, j, t: (layer, d, j, 0, 0)),
            pl.BlockSpec((None, None, None, kk + sb, kk), lambda d, j, t: (layer, d, j, 0, 0)),
            pl.BlockSpec((None, None, None, 2, sb // 2), lambda d, j, t: (layer, d, j, 0, 0)),
            pl.BlockSpec((None, None, b, sb), lambda d, j, t: (d, j, 0, 0)),
        ],
        out_specs=[
            pl.BlockSpec((None, chunks, SSM_CHUNK, b, LANES), lambda d, j, t: (d, tile(d, t), 0, 0, j)),
            pl.BlockSpec((None, None, b, sb), lambda d, j, t: (d, j, 0, 0)),
        ],
        out_shape=[jax.ShapeDtypeStruct(y_shape, F32), jax.ShapeDtypeStruct(h0.shape, F32)],
        scratch_shapes=[
            pltpu.VMEM((chunks, b, sb), F32),
            pltpu.VMEM((chunks, b, sb), F32),
            pltpu.VMEM((b, sb), F32),
        ],
        compiler_params=_params("arbitrary", "arbitrary", "arbitrary"),
        name="ssm",
    )(u5, m_out, m_y, lam8, h0)


def _ssm_matrices(a_re, a_im, log_dt, b_re, b_im, c_re, c_im, reverse):
    g, p = a_re.shape
    h = b_re.shape[-1]
    t_n = SSM_CHUNK
    gpb = LANES // h
    nb = g // gpb
    a_re = a_re.astype(F32)
    a_im = a_im.astype(F32)
    dt = jnp.exp(log_dt.astype(F32))[:, None]
    mag = jnp.exp(a_re * dt)
    lam_re = mag * jnp.cos(a_im * dt)
    lam_im = mag * jnp.sin(a_im * dt)
    denom = a_re * a_re + a_im * a_im
    nr, ni = lam_re - 1.0, lam_im
    f_re = ((nr * a_re + ni * a_im) / denom)[..., None]
    f_im = ((ni * a_re - nr * a_im) / denom)[..., None]
    b_re = b_re.astype(F32)
    b_im = b_im.astype(F32)
    bb_re = f_re * b_re - f_im * b_im
    bb_im = f_re * b_im + f_im * b_re
    c_re = c_re.astype(F32)
    c_im = c_im.astype(F32)
    pr, pi = [jnp.ones_like(lam_re)], [jnp.zeros_like(lam_im)]
    for _ in range(t_n):
        pr, pi = (pr + [pr[-1] * lam_re - pi[-1] * lam_im], pi + [pr[-1] * lam_im + pi[-1] * lam_re])
    pr = jnp.stack(pr)
    pi = jnp.stack(pi)

    q_re = pr[:t_n, :, :, None] * bb_re - pi[:t_n, :, :, None] * bb_im
    q_im = pr[:t_n, :, :, None] * bb_im + pi[:t_n, :, :, None] * bb_re
    hi = lax.Precision.HIGHEST
    taps = (jnp.einsum('gop,kgpi->kgoi', c_re, q_re, precision=hi)
            - jnp.einsum('gop,kgpi->kgoi', c_im, q_im, precision=hi))
    lag = jnp.arange(t_n)[None, :] - jnp.arange(t_n)[:, None]
    toep = jnp.where((lag >= 0)[:, :, None, None, None], taps[jnp.clip(lag, 0, t_n - 1)], 0.0)
    toep = toep.reshape(t_n, t_n, nb, gpb, h, h)
    cr, ci = c_re[None], c_im[None]
    lr, li = pr[1:, :, None, :], pi[1:, :, None, :]
    w_in = jnp.stack([cr * lr - ci * li, -(cr * li + ci * lr)])
    w_in = w_in.reshape(2, t_n, nb, gpb, h, p)
    w_out = jnp.stack([q_re[::-1], q_im[::-1]]).reshape(2, t_n, nb, gpb, p, h)
    if reverse:
        toep = toep[::-1, ::-1]
        w_in = w_in[:, ::-1]
        w_out = w_out[:, ::-1]
    k_dim = t_n * gpb * h
    sb = 2 * gpb * p

    def on_group_diagonal(compact, row_width, col_width, n_cols):
        n_rows, n_compact = compact.shape[1:]
        col = jnp.arange(n_cols)
        src = (col // (gpb * col_width)) * col_width + col % col_width
        place = (jnp.arange(n_compact)[:, None] == src[None, :]).astype(F32)
        full = jnp.einsum('jrk,kc->jrc', compact, place)
        row_group = (jnp.arange(n_rows) // row_width) % gpb
        col_group = (col // col_width) % gpb
        return jnp.where(row_group[:, None] == col_group[None, :], full, 0.0).astype(BF16)

    m_toep = on_group_diagonal(toep.transpose(2, 0, 3, 5, 1, 4).reshape(nb, k_dim, t_n * h), h, h, k_dim)
    m_in = on_group_diagonal(w_in.transpose(2, 0, 3, 5, 1, 4).reshape(nb, sb, t_n * h), p, h, k_dim)
    m_out = on_group_diagonal(w_out.transpose(2, 1, 3, 5, 0, 4).reshape(nb, k_dim, 2 * p), h, p, sb)
    m_y = jnp.concatenate([m_toep, m_in], axis=1)
    lam8 = jnp.stack([pr[t_n].reshape(nb, gpb * p), pi[t_n].reshape(nb, gpb * p)], axis=1)
    return m_out, m_y, lam8


def _mix_kernel(um_ref, up_ref, un_ref, us_ref, yl_ref, yc_ref, x_ref, gate_ref, sh_ref, sc_ref,
                sd_ref, wg_ref, wo_ref, gp_ref, gf_ref, o_ref, h_ref, ext_ref, cat_ref, *,
                n_lat_tiles, seq, ctx):
    tt, b, pw_all = um_ref.shape
    pg = pw_all // len(POOL_WINDOWS)
    sw = us_ref.shape[-1]
    rows = tt * b
    i = pl.program_id(0)
    is_ctx = i >= n_lat_tiles
    t0 = jnp.where(is_ctx, i - n_lat_tiles, i) * tt
    seg = jnp.where(is_ctx, ctx, seq)
    has_prev = (t0 > 0).astype(F32)
    has_next = (t0 + tt < seg).astype(F32)
    hl = POOL_HALO
    ext_ref[0:hl] = up_ref[...] * has_prev
    ext_ref[hl:hl + tt] = um_ref[...]
    ext_ref[hl + tt:hl + tt + hl] = un_ref[...] * has_next

    tl = t0 + lax.broadcasted_iota(jnp.int32, (tt, b, pg), 0)
    for g, w in enumerate(POOL_WINDOWS):
        sl = slice(g * pg, (g + 1) * pg)
        lo = hl - w // 2
        acc = ext_ref[lo:lo + tt, :, sl]
        for k in range(1, w):
            acc = acc + ext_ref[lo + k:lo + k + tt, :, sl]
        cnt = (jnp.minimum(tl + w // 2, seg) - jnp.maximum(tl - w // 2, 0)).astype(F32)
        p = acc / cnt - ext_ref[hl:hl + tt, :, sl]
        cat_ref[:, sl] = p.reshape(rows, pg).astype(BF16)

    y = jnp.where(is_ctx, yc_ref[0] + yc_ref[1], yl_ref[0] + yl_ref[1])
    yv = y.reshape(rows, sw) + sd_ref[...] * us_ref[...].reshape(rows, sw)
    yv = jax.nn.gelu(yv)
    glu = jax.nn.sigmoid(jnp.dot(yv.astype(BF16), wg_ref[...], preferred_element_type=F32))
    cat_ref[:, pw_all:] = (yv * glu).astype(BF16)

    mix = jnp.dot(cat_ref[...], wo_ref[...], preferred_element_type=F32)
    x_new = x_ref[...] + gate_ref[...] * _rms(mix, gp_ref[...]).reshape(o_ref.shape)
    o_ref[...] = x_new
    h = _rms(x_new, gf_ref[...]) * (1.0 + sc_ref[...]) + sh_ref[...]
    h_ref[...] = h.reshape(h_ref.shape).astype(BF16)


def _mix(u, y_lat, y_ctx, xs, mod, ssm_d, w_glu, w_out, g_post, g_ffn, layer, tt, n_tiles,
         n_lat_tiles, seq, ctx):
    s, b, d = xs.shape
    mixw = u.shape[-1]
    sw = y_lat.shape[-1]
    pw_all = mixw - sw
    hl = POOL_HALO
    hpt = tt // hl
    last_halo = n_tiles * hpt - 1
    sel = lambda i: jnp.where(i >= n_lat_tiles, 1, 0)
    return pl.pallas_call(
        functools.partial(_mix_kernel, n_lat_tiles=n_lat_tiles, seq=seq, ctx=ctx),
        grid=(n_tiles,),
        in_specs=[
            pl.BlockSpec((tt, b, pw_all), lambda i: (i, 0, 0)),
            pl.BlockSpec((hl, b, pw_all), lambda i: (jnp.maximum(i * hpt - 1, 0), 0, 0)),
            pl.BlockSpec((hl, b, pw_all), lambda i: (jnp.minimum((i + 1) * hpt, last_halo), 0, 0)),
            pl.BlockSpec((tt, b, sw), lambda i: (i, 0, pw_all // sw)),
            pl.BlockSpec((2, tt, b, sw), lambda i: (0, jnp.minimum(i, n_lat_tiles - 1), 0, 0)),
            pl.BlockSpec((2, tt, b, sw), lambda i: (0, jnp.maximum(i - n_lat_tiles, 0), 0, 0)),
            pl.BlockSpec((tt, b, d), lambda i: (i, 0, 0)),
            pl.BlockSpec((None, None, b, d), lambda i: (layer, sel(i), 0, 2)),
            pl.BlockSpec((None, None, b, d), lambda i: (layer, sel(i), 0, 3)),
            pl.BlockSpec((None, None, b, d), lambda i: (layer, sel(i), 0, 4)),
            _const_spec((1, sw)),
            _const_spec(w_glu.shape[1:], layer),
            _const_spec(w_out.shape[1:], layer),
            _const_spec((1, d)),
            _const_spec((1, d)),
        ],
        out_specs=[pl.BlockSpec((tt, b, d), lambda i: (i, 0, 0)),
                   pl.BlockSpec((tt * b, d), lambda i: (i, 0))],
        out_shape=[jax.ShapeDtypeStruct(xs.shape, F32),
                   jax.ShapeDtypeStruct((n_tiles * tt * b, d), BF16)],
        scratch_shapes=[
            pltpu.VMEM((tt + 2 * hl, b, pw_all), F32),
            pltpu.VMEM((tt * b, mixw), BF16),
        ],
        input_output_aliases={6: 0},
        compiler_params=_params("arbitrary"),
        name="mix",
    )(u, u, u, u, y_lat, y_ctx, xs, mod, mod, mod, ssm_d, w_glu, w_out, g_post, g_ffn)


def _ffn_up_kernel(h_ref, w_ref, z_ref):
    z = jnp.dot(h_ref[...], w_ref[...], preferred_element_type=F32).astype(BF16)
    fb = z_ref.shape[-1]
    for k in range(z_ref.shape[0]):
        z_ref[k] = z[:, k * fb:(k + 1) * fb]


def _ffn_up(h, w_up, layer, tm, nb, fb):
    rows, d = h.shape
    n = w_up.shape[-1]
    return pl.pallas_call(
        _ffn_up_kernel,
        grid=(rows // tm, n // nb),
        in_specs=[
            pl.BlockSpec((tm, d), lambda i, j: (i, 0)),
            pl.BlockSpec((None, d, nb), lambda i, j: (layer, 0, j)),
        ],
        out_specs=pl.BlockSpec((nb // fb, tm, fb), lambda i, j: (j, i, 0)),
        out_shape=jax.ShapeDtypeStruct((n // fb, rows, fb), BF16),
        compiler_params=_params("arbitrary", "arbitrary"),
        name="ffn_up",
    )(h, w_up)


FFN_QUARTERS = 4
CONV_PIECE_STEPS = 4


def _ffn_down_kernel(va_ref, vm_ref, vb_ref, ga_ref, gm_ref, gb_ref, kv_ref, kg_ref, wd_ref, x_ref,
                     gate_ref, gp_ref, o_ref, *scratch, n_lat_tiles, n_ctx_tiles, batch_major_out):
    acc_ref = scratch[0] if batch_major_out else None
    (act_ref, kb_ref, eva_ref, evm0_ref, evm1_ref, evb_ref, ega_ref, egm0_ref, egm1_ref,
     egb_ref) = scratch[1:] if batch_major_out else scratch
    w, b, fb = evm0_ref.shape[0] - 2, evm0_ref.shape[1], evm0_ref.shape[2]
    i = pl.program_id(0)
    j = pl.program_id(1)
    nf = pl.num_programs(1) - 1
    is_ctx = i >= n_lat_tiles
    is_lat = jnp.logical_not(is_ctx)
    ic = i - n_lat_tiles
    lat_f = is_lat.astype(F32)
    ctx_f = is_ctx.astype(F32)
    vert_up = jnp.logical_and(is_lat, i > 0).astype(F32)
    vert_dn = jnp.logical_and(is_lat, i < n_lat_tiles - 1).astype(F32)
    hz_prev = jnp.logical_and(is_ctx, ic > 0).astype(F32)
    hz_next = jnp.logical_and(is_ctx, ic < n_ctx_tiles - 1).astype(F32)
    slot = j % 2

    @pl.when(j == 0)
    def _():
        if batch_major_out:
            acc_ref[...] = jnp.zeros(acc_ref.shape, F32)
        else:
            o_ref[...] = jnp.zeros(o_ref.shape, F32)
        act_ref[1] = jnp.zeros(act_ref.shape[1:], BF16)

    streams = ((va_ref, vm_ref, vb_ref, (eva_ref, evm0_ref, evm1_ref, evb_ref), kv_ref),
               (ga_ref, gm_ref, gb_ref, (ega_ref, egm0_ref, egm1_ref, egb_ref), kg_ref))
    pair = 2 * b
    zero_row = jnp.zeros((1, b, fb), F32)
    edge = lambda ref, row0, part: ref[row0:row0 + pair, :].astype(F32)[part * b:(part + 1) * b]
    for which, (a_ref, m_ref, b_ref, (ea_ref, em0_ref, em1_ref, eb_ref), k_ref) in enumerate(streams):
        ea_ref[0:1] = zero_row
        ea_ref[w + 1:w + 2] = zero_row
        eb_ref[0:1] = zero_row
        eb_ref[w + 1:w + 2] = zero_row
        em0_ref[0:1] = (edge(a_ref, (w - 2) * b, 1) * hz_prev).reshape(1, b, fb)
        em0_ref[w + 1:w + 2] = (edge(m_ref, w * b, 0) * ctx_f).reshape(1, b, fb)
        em1_ref[0:1] = (edge(m_ref, (w - 2) * b, 1) * ctx_f).reshape(1, b, fb)
        em1_ref[w + 1:w + 2] = (edge(b_ref, 0, 0) * hz_next).reshape(1, b, fb)
        for row, flags in enumerate(((vert_up, None, lat_f), (lat_f, None, vert_dn))):
            for r, flag in enumerate(flags):
                for dc in range(3):
                    kr = k_ref[3 * r + dc:3 * r + dc + 1, :]
                    kr = kr if flag is None else kr * flag
                    kb_ref[18 * which + 9 * row + 3 * r + dc] = jnp.broadcast_to(kr, (b, fb))

    tq = w // FFN_QUARTERS
    nq_lanes = x_ref.shape[-1] // FFN_QUARTERS
    tp = CONV_PIECE_STEPS

    def stage(q):
        lo, hi = q * tq * b, (q + 1) * tq * b
        for a_ref, m_ref, b_ref, (ea_ref, em0_ref, em1_ref, eb_ref), _ in streams:
            for src, off, dst in ((a_ref, 0, ea_ref), (m_ref, 0, em0_ref), (m_ref, w * b, em1_ref),
                                  (b_ref, 0, eb_ref)):
                blk = src[off + lo:off + hi, :].astype(F32)
                dst[1 + q * tq:1 + (q + 1) * tq] = blk.reshape(tq, b, fb)

    def conv_piece(row, t0, lanes, which):
        out = None
        for r, e_ref in enumerate(streams[which][3][row:row + 3]):
            for dc in range(3):
                tap = kb_ref[18 * which + 9 * row + 3 * r + dc, :, lanes]
                term = e_ref[t0 + dc:t0 + dc + tp, :, lanes] * tap
                out = term if out is None else out + term
        return out

    def project(q):
        cols = slice(q * nq_lanes, (q + 1) * nq_lanes)
        part = jnp.dot(act_ref[1 - slot], wd_ref[:, cols], preferred_element_type=F32)
        if batch_major_out:
            acc_ref[:, cols] += part
        else:
            o_ref[:, :, cols] += part.reshape(2 * w, b, nq_lanes)

    @pl.when(j < nf)
    def _():
        stage(0)
        for q in range(FFN_QUARTERS):
            if q + 1 < FFN_QUARTERS:
                stage(q + 1)
            project(q)
            for row in range(2):
                for lc in range(fb // LANES):
                    lanes = slice(lc * LANES, (lc + 1) * LANES)
                    for t0 in range(q * tq, (q + 1) * tq, tp):
                        val = conv_piece(row, t0, lanes, 0)
                        gat = conv_piece(row, t0, lanes, 1)
                        act = (val * jax.nn.silu(gat)).reshape(tp * b, LANES).astype(BF16)
                        act_ref[slot, (row * w + t0) * b:(row * w + t0 + tp) * b, lanes] = act

    @pl.when(j == nf)
    def _():
        for q in range(FFN_QUARTERS):
            project(q)
        if batch_major_out:
            out = x_ref[...] + gate_ref[...] * _rms(acc_ref[...], gp_ref[...]).reshape(x_ref.shape)
            for t in range(out.shape[0]):
                o_ref[:, t, :] = out[t]
        else:
            o_ref[...] = x_ref[...] + gate_ref[...] * _rms(o_ref[...], gp_ref[...])


def _ffn_down(z, w_conv9, w_down, xs, mod, g_post, layer, fb, n_tiles, n_lat_tiles, n_ctx_tiles,
              batch_major_out):
    s, b, d = xs.shape
    f = w_down.shape[1]
    nf = f // fb
    tt = 2 * GRID_W
    rows = tt * b
    half = rows // 2
    up = lambda i: jnp.maximum(2 * i - 1, 0)
    dn = lambda i: jnp.minimum(2 * i + 2, 2 * n_tiles - 1)
    sel = lambda i: jnp.where(i >= n_lat_tiles, 1, 0)
    cur = lambda j: jnp.minimum(j, nf - 1)
    prv = lambda j: jnp.maximum(j - 1, 0)
    zspec = lambda n_rows, row, col0: pl.BlockSpec((None, n_rows, fb),
                                                   lambda i, j: (col0 + cur(j), row(i), 0))
    same = lambda i: i
    assert GRID_W % (FFN_QUARTERS * SUBLANES) == 0 and d % (FFN_QUARTERS * LANES) == 0
    return pl.pallas_call(
        functools.partial(_ffn_down_kernel, n_lat_tiles=n_lat_tiles, n_ctx_tiles=n_ctx_tiles,
                          batch_major_out=batch_major_out),
        grid=(n_tiles, nf + 1),
        in_specs=[
            zspec(half, up, 0), zspec(rows, same, 0), zspec(half, dn, 0),
            zspec(half, up, nf), zspec(rows, same, nf), zspec(half, dn, nf),
            pl.BlockSpec((9, fb), lambda i, j: (0, cur(j))),
            pl.BlockSpec((9, fb), lambda i, j: (0, nf + cur(j))),
            pl.BlockSpec((None, fb, d), lambda i, j: (layer, prv(j), 0)),
            pl.BlockSpec((tt, b, d), lambda i, j: (i, 0, 0)),
            pl.BlockSpec((None, None, b, d), lambda i, j: (layer, sel(i), 0, 5)),
            _const_spec((1, d)),
        ],
        out_specs=(pl.BlockSpec((b, tt, d), lambda i, j: (0, i, 0)) if batch_major_out
                   else pl.BlockSpec((tt, b, d), lambda i, j: (i, 0, 0))),
        out_shape=jax.ShapeDtypeStruct((b, n_tiles * tt, d) if batch_major_out else xs.shape, F32),
        scratch_shapes=([pltpu.VMEM((rows, d), F32)] if batch_major_out else []) + [
            pltpu.VMEM((2, rows, fb), BF16),
            pltpu.VMEM((36, b, fb), F32),
        ] + [pltpu.VMEM((GRID_W + 2, b, fb), F32)] * 8,
        input_output_aliases={} if batch_major_out else {9: 0},
        compiler_params=_params("arbitrary", "arbitrary"),
        name="ffn_down",
    )(z, z, z, z, z, z, w_conv9, w_conv9, w_down, xs, mod, g_post)


def _pick(limit, total):
    t = min(limit, total)
    while total % t:
        t -= 1
    return t


def kernel(x, c, ctx, c_ctx, w_ada, b_ada, w_in, w_pool, pool_scale, ssm_a_re, ssm_a_im, ssm_log_dt,
           ssm_b_re, ssm_b_im, ssm_c_re, ssm_c_im, ssm_d, w_glu, w_out, g_pre_mix, g_post_mix,
           g_pre_ffn, g_post_ffn, w_up, w_conv, w_down):
    bsz, seq, d = x.shape
    n_ctx = ctx.shape[1]
    depth = w_ada.shape[0]
    mixw = w_in.shape[-1]
    sw = ssm_d.shape[-1]
    f = w_down.shape[1]
    assert bsz == SUBLANES and seq % GRID_W == 0 and n_ctx % GRID_W == 0
    assert sw % LANES == 0 and (mixw - sw) % sw == 0 and LANES % ssm_b_re.shape[-1] == 0
    s_all = seq + n_ctx
    tt = GRID_W
    n_lat_tiles = seq // tt
    n_ctx_tiles = n_ctx // tt
    n_tiles = n_lat_tiles + n_ctx_tiles
    tt_up = 2 * GRID_W
    assert seq % tt_up == 0 and n_ctx % tt_up == 0
    nb_up = _pick(1024, 2 * f)
    fb_in_place = _pick(512, f)
    fb_fresh = _pick(256, f)
    lat_chunks = _pick(128, seq // SSM_CHUNK)
    ctx_chunks = _pick(64, n_ctx // SSM_CHUNK)
    assert (seq // SSM_CHUNK) % ctx_chunks == 0

    c16 = jnp.concatenate([c, jnp.broadcast_to(c_ctx[None, :], (SUBLANES, d))], axis=0)
    mod_all = _ada_all(c16, w_ada, b_ada, _pick(1024, w_ada.shape[-1]))

    row = lambda v: v.reshape(1, -1).astype(F32)
    n_pool = len(POOL_WINDOWS)
    pw_all = mixw - sw
    pg = pw_all // n_pool
    w_pool_scaled = w_pool.astype(F32) * pool_scale.astype(F32).reshape(depth, n_pool, 1, pg)
    w_out_pool = jnp.einsum('lgio,lgod->lgid', w_pool_scaled,
                            w_out[:, :pw_all].astype(F32).reshape(depth, n_pool, pg, d),
                            precision=lax.Precision.HIGHEST).reshape(depth, pw_all, d)
    w_out = jnp.concatenate([w_out_pool, w_out[:, pw_all:].astype(F32)], axis=1)
    w_in, w_glu, w_out, w_up, w_down = (v.astype(BF16) for v in (w_in, w_glu, w_out, w_up, w_down))
    ssm_params = (ssm_a_re, ssm_a_im, ssm_log_dt, ssm_b_re, ssm_b_im, ssm_c_re, ssm_c_im)
    mats = [jax.vmap(functools.partial(_ssm_matrices, reverse=dr == 1))(*(v[:, dr] for v in ssm_params))
            for dr in range(2)]
    mats = tuple(jnp.stack(m, axis=1) for m in zip(*mats))
    nbk = sw // LANES
    lane0 = (mixw - sw) // LANES
    h0 = jnp.zeros((2, nbk, bsz, mats[0].shape[-1]), F32)
    for l in range(depth):
        last = l == depth - 1
        if l == 0:
            xs, u = _premix_first(x, ctx, mod_all, row(g_pre_mix[l]), w_in, l, tt)
        else:
            u = _premix(xs, mod_all, row(g_pre_mix[l]), w_in, l, tt, n_lat_tiles)
        u5 = u.reshape(s_all // SSM_CHUNK, SSM_CHUNK, bsz, mixw)
        y_ctx, h_ctx = _ssm(u5, mats, l, h0, ctx_chunks, (seq // SSM_CHUNK) // ctx_chunks,
                            (n_ctx // SSM_CHUNK) // ctx_chunks, lane0)
        y_lat, _ = _ssm(u5, mats, l, h_ctx, lat_chunks, 0, (seq // SSM_CHUNK) // lat_chunks, lane0)

        nt = n_lat_tiles if last else n_tiles
        xs, h_ffn = _mix(u, y_lat.reshape(2, seq, bsz, sw), y_ctx.reshape(2, n_ctx, bsz, sw), xs,
                         mod_all, row(ssm_d[l]), w_glu, w_out, row(g_post_mix[l]), row(g_pre_ffn[l]), l,
                         tt, nt, n_lat_tiles, seq, n_ctx)
        nt_up = (seq if last else s_all) // tt_up
        fb = fb_fresh if last else fb_in_place
        z = _ffn_up(h_ffn, w_up, l, _pick(2048, math.gcd(seq, n_ctx) * bsz), nb_up, fb)
        xs = _ffn_down(z, w_conv[l].reshape(9, 2 * f).astype(F32), w_down, xs, mod_all,
                       row(g_post_ffn[l]), l, fb, nt_up, seq // tt_up, n_ctx // tt_up, last)
    return xs
```

```python
import functools
import math

import jax
import jax.numpy as jnp
from jax import lax
from jax.experimental import pallas as pl
from jax.experimental.pallas import tpu as pltpu

GRID_W = 64
POOL_WINDOWS = (2, 4, 8, 16)
POOL_HALO = max(POOL_WINDOWS) // 2
EPS = 1e-6
SUBLANES = 8
LANES = 128
SSM_CHUNK = 8
VMEM_LIMIT_BYTES = 62 * 1024 * 1024

F32 = jnp.float32
BF16 = jnp.bfloat16


def _params(*sem):
    return pltpu.CompilerParams(dimension_semantics=sem, vmem_limit_bytes=VMEM_LIMIT_BYTES)


def _const_spec(shape, layer=None):
    zeros = (0,) * len(shape)
    if layer is None:
        return pl.BlockSpec(shape, lambda *_: zeros, pipeline_mode=pl.Buffered(1))
    return pl.BlockSpec((None,) + tuple(shape), lambda *_: (layer,) + zeros, pipeline_mode=pl.Buffered(1))


def _rms(v, gain):
    ms = jnp.mean(v * v, axis=-1, keepdims=True)
    return v * lax.rsqrt(ms + EPS) * gain


def _ada_kernel(c_ref, w_ref, b_ref, o_ref):
    s = jax.nn.silu(c_ref[...]).astype(BF16)
    r = jnp.dot(s, w_ref[...].astype(BF16), preferred_element_type=F32) + b_ref[...]
    o_ref[...] = r.reshape(o_ref.shape)


def _ada_all(c16, w_ada, b_ada, nb):
    depth, d, n = w_ada.shape
    return pl.pallas_call(
        _ada_kernel,
        grid=(depth, n // nb),
        in_specs=[
            pl.BlockSpec((2 * SUBLANES, d), lambda l, j: (0, 0)),
            pl.BlockSpec((None, d, nb), lambda l, j: (l, 0, j)),
            pl.BlockSpec((None, 1, nb), lambda l, j: (l, 0, j)),
        ],
        out_specs=pl.BlockSpec((None, 2, SUBLANES, nb), lambda l, j: (l, 0, 0, j)),
        out_shape=jax.ShapeDtypeStruct((depth, 2, SUBLANES, n), F32),
        compiler_params=_params("arbitrary", "arbitrary"),
        name="ada",
    )(c16, w_ada, b_ada.reshape(depth, 1, n))


def _premix_kernel(x_ref, sh_ref, sc_ref, g_ref, w_ref, u_ref):
    tt, b, d = x_ref.shape
    h = _rms(x_ref[...], g_ref[...]) * (1.0 + sc_ref[...]) + sh_ref[...]
    u = jnp.dot(h.reshape(tt * b, d).astype(BF16), w_ref[...], preferred_element_type=F32)
    u_ref[...] = u.reshape(u_ref.shape)


def _premix_first_kernel(xl_ref, xc_ref, sh_ref, sc_ref, g_ref, w_ref, xs_ref, u_ref, *, n_lat_tiles):
    tt = xs_ref.shape[0]
    is_ctx = pl.program_id(0) >= n_lat_tiles

    @pl.when(is_ctx)
    def _():
        for t in range(tt):
            xs_ref[t] = xc_ref[:, t, :]

    @pl.when(jnp.logical_not(is_ctx))
    def _():
        for t in range(tt):
            xs_ref[t] = xl_ref[:, t, :]

    _premix_kernel(xs_ref, sh_ref, sc_ref, g_ref, w_ref, u_ref)


def _premix_first(x, ctx, mod, g, w, layer, tt):
    b, seq, d = x.shape
    n_ctx = ctx.shape[1]
    n = w.shape[-1]
    n_lat_tiles = seq // tt
    sel = lambda i: jnp.where(i >= n_lat_tiles, 1, 0)
    stream = jax.ShapeDtypeStruct((seq + n_ctx, b, d), F32)
    return pl.pallas_call(
        functools.partial(_premix_first_kernel, n_lat_tiles=n_lat_tiles),
        grid=((seq + n_ctx) // tt,),
        in_specs=[
            pl.BlockSpec((b, tt, d), lambda i: (0, jnp.minimum(i, n_lat_tiles - 1), 0)),
            pl.BlockSpec((b, tt, d), lambda i: (0, jnp.maximum(i - n_lat_tiles, 0), 0)),
            pl.BlockSpec((None, None, b, d), lambda i: (layer, sel(i), 0, 0)),
            pl.BlockSpec((None, None, b, d), lambda i: (layer, sel(i), 0, 1)),
            _const_spec((1, d)),
            _const_spec((d, n), layer),
        ],
        out_specs=[pl.BlockSpec((tt, b, d), lambda i: (i, 0, 0)), pl.BlockSpec((tt, b, n), lambda i: (i, 0, 0))],
        out_shape=[stream, jax.ShapeDtypeStruct((seq + n_ctx, b, n), F32)],
        compiler_params=_params("arbitrary"),
        name="premix_first",
    )(x, ctx, mod, mod, g, w)


def _premix(xs, mod, g, w, layer, tt, n_lat_tiles):
    s, b, d = xs.shape
    n = w.shape[-1]
    sel = lambda i: jnp.where(i >= n_lat_tiles, 1, 0)
    return pl.pallas_call(
        _premix_kernel,
        grid=(s // tt,),
        in_specs=[
            pl.BlockSpec((tt, b, d), lambda i: (i, 0, 0)),
            pl.BlockSpec((None, None, b, d), lambda i: (layer, sel(i), 0, 0)),
            pl.BlockSpec((None, None, b, d), lambda i: (layer, sel(i), 0, 1)),
            _const_spec((1, d)),
            _const_spec((d, n), layer),
        ],
        out_specs=pl.BlockSpec((tt, b, n), lambda i: (i, 0, 0)),
        out_shape=jax.ShapeDtypeStruct((s, b, n), F32),
        compiler_params=_params("arbitrary"),
        name="premix",
    )(xs, mod, mod, g, w)


def _ssm_kernel(u_ref, mo_ref, my_ref, lam_ref, h0_ref, y_ref, hf_ref, s_ref, hin_ref, hc_ref, *,
                chunks):
    c_n = chunks
    sb = hc_ref.shape[-1]
    half = sb // 2
    d = pl.program_id(0)
    t = pl.program_id(2)

    @pl.when(t == 0)
    def _():
        hc_ref[...] = h0_ref[...]

    step_of = lambda k: jnp.where(d == 0, k, SSM_CHUNK - 1 - k)
    u_rows = jnp.concatenate(
        [u_ref[:, pl.ds(step_of(k), 1)].reshape(c_n * SUBLANES, LANES).astype(BF16)
         for k in range(SSM_CHUNK)], axis=-1)
    s_ref[...] = jnp.dot(u_rows, mo_ref[...], preferred_element_type=F32).reshape(s_ref.shape)

    lre = jnp.broadcast_to(lam_ref[0:1, :], (SUBLANES, half))
    lim = jnp.broadcast_to(lam_ref[1:2, :], (SUBLANES, half))

    def step(i, carry):
        hr, hi = carry
        c = jnp.where(d == 0, i, c_n - 1 - i)
        hin_ref[c, :, :half] = hr
        hin_ref[c, :, half:] = hi
        sr = s_ref[c, :, :half]
        si = s_ref[c, :, half:]
        return lre * hr - lim * hi + sr, lre * hi + lim * hr + si

    hr, hi = lax.fori_loop(0, c_n, step, (hc_ref[:, :half], hc_ref[:, half:]))
    hc_ref[:, :half] = hr
    hc_ref[:, half:] = hi
    hf_ref[:, :half] = hr
    hf_ref[:, half:] = hi

    h_rows = hin_ref[...].reshape(c_n * SUBLANES, sb).astype(BF16)
    y = jnp.dot(jnp.concatenate([u_rows, h_rows], axis=-1), my_ref[...], preferred_element_type=F32)
    for k in range(SSM_CHUNK):
        y_ref[:, pl.ds(step_of(k), 1)] = y[:, k * LANES:(k + 1) * LANES].reshape(c_n, 1, SUBLANES, LANES)


def _ssm(u5, mats, layer, h0, chunks, block_off, n_tiles, ssm_lane_block0):
    m_out, m_y, lam8 = mats
    b = u5.shape[2]
    _, _, nb, kk, sb = m_out.shape
    tile = lambda d, t: jnp.where(d == 0, t, n_tiles - 1 - t)
    y_shape = (2, n_tiles * chunks, SSM_CHUNK, b, nb * LANES)
    return pl.pallas_call(
        functools.partial(_ssm_kernel, chunks=chunks),
        grid=(2, nb, n_tiles),
        in_specs=[
            pl.BlockSpec((chunks, SSM_CHUNK, b, LANES),
                         lambda d, j, t: (tile(d, t) + block_off, 0, 0, ssm_lane_block0 + j)),
            pl.BlockSpec((None, None, None, kk, sb), lambda d, j, t: (layer, d, j, 0, 0)),
            pl.BlockSpec((None, None, None, kk + sb, kk), lambda d, j, t: (layer, d, j, 0, 0)),
            pl.BlockSpec((None, None, None, 2, sb // 2), lambda d, j, t: (layer, d, j, 0, 0)),
            pl.BlockSpec((None, None, b, sb), lambda d, j, t: (d, j, 0, 0)),
        ],
        out_specs=[
            pl.BlockSpec((None, chunks, SSM_CHUNK, b, LANES), lambda d, j, t: (d, tile(d, t), 0, 0, j)),
            pl.BlockSpec((None, None, b, sb), lambda d, j, t: (d, j, 0, 0)),
        ],
        out_shape=[jax.ShapeDtypeStruct(y_shape, F32), jax.ShapeDtypeStruct(h0.shape, F32)],
        scratch_shapes=[
            pltpu.VMEM((chunks, b, sb), F32),
            pltpu.VMEM((chunks, b, sb), F32),
            pltpu.VMEM((b, sb), F32),
        ],
        compiler_params=_params("arbitrary", "arbitrary", "arbitrary"),
        name="ssm",
    )(u5, m_out, m_y, lam8, h0)


def _ssm_matrices(a_re, a_im, log_dt, b_re, b_im, c_re, c_im):
    g, p = a_re.shape
    h = b_re.shape[-1]
    t_n = SSM_CHUNK
    gpb = LANES // h
    nb = g // gpb
    a_re = a_re.astype(F32)
    a_im = a_im.astype(F32)
    dt = jnp.exp(log_dt.astype(F32))[:, None]
    mag = jnp.exp(a_re * dt)
    lam_re = mag * jnp.cos(a_im * dt)
    lam_im = mag * jnp.sin(a_im * dt)
    denom = a_re * a_re + a_im * a_im
    nr, ni = lam_re - 1.0, lam_im
    f_re = ((nr * a_re + ni * a_im) / denom)[..., None]
    f_im = ((ni * a_re - nr * a_im) / denom)[..., None]
    b_re = b_re.astype(F32)
    b_im = b_im.astype(F32)
    bb_re = f_re * b_re - f_im * b_im
    bb_im = f_re * b_im + f_im * b_re
    c_re = c_re.astype(F32)
    c_im = c_im.astype(F32)
    pr, pi = [jnp.ones_like(lam_re)], [jnp.zeros_like(lam_im)]
    for _ in range(t_n):
        pr, pi = (pr + [pr[-1] * lam_re - pi[-1] * lam_im], pi + [pr[-1] * lam_im + pi[-1] * lam_re])
    pr = jnp.stack(pr)
    pi = jnp.stack(pi)

    q_re = pr[:t_n, :, :, None] * bb_re - pi[:t_n, :, :, None] * bb_im
    q_im = pr[:t_n, :, :, None] * bb_im + pi[:t_n, :, :, None] * bb_re
    hi = lax.Precision.HIGHEST
    taps = (jnp.einsum('gop,kgpi->kgoi', c_re, q_re, precision=hi)
            - jnp.einsum('gop,kgpi->kgoi', c_im, q_im, precision=hi))
    lag = jnp.arange(t_n)[None, :] - jnp.arange(t_n)[:, None]
    toep = jnp.where((lag >= 0)[:, :, None, None, None], taps[jnp.clip(lag, 0, t_n - 1)], 0.0)
    toep = toep.reshape(t_n, t_n, nb, gpb, h, h)
    cr, ci = c_re[None], c_im[None]
    lr, li = pr[1:, :, None, :], pi[1:, :, None, :]
    w_in = jnp.stack([cr * lr - ci * li, -(cr * li + ci * lr)])
    w_in = w_in.reshape(2, t_n, nb, gpb, h, p)
    w_out = jnp.stack([q_re[::-1], q_im[::-1]]).reshape(2, t_n, nb, gpb, p, h)
    k_dim = t_n * gpb * h
    sb = 2 * gpb * p

    def on_group_diagonal(compact, row_width, col_width, n_cols):
        n_rows, n_compact = compact.shape[1:]
        col = jnp.arange(n_cols)
        src = (col // (gpb * col_width)) * col_width + col % col_width
        place = (jnp.arange(n_compact)[:, None] == src[None, :]).astype(F32)
        full = jnp.einsum('jrk,kc->jrc', compact, place)
        row_group = (jnp.arange(n_rows) // row_width) % gpb
        col_group = (col // col_width) % gpb
        return jnp.where(row_group[:, None] == col_group[None, :], full, 0.0).astype(BF16)

    m_toep = on_group_diagonal(toep.transpose(2, 0, 3, 5, 1, 4).reshape(nb, k_dim, t_n * h), h, h, k_dim)
    m_in = on_group_diagonal(w_in.transpose(2, 0, 3, 5, 1, 4).reshape(nb, sb, t_n * h), p, h, k_dim)
    m_out = on_group_diagonal(w_out.transpose(2, 1, 3, 5, 0, 4).reshape(nb, k_dim, 2 * p), h, p, sb)
    m_y = jnp.concatenate([m_toep, m_in], axis=1)
    lam8 = jnp.stack([pr[t_n].reshape(nb, gpb * p), pi[t_n].reshape(nb, gpb * p)], axis=1)
    return m_out, m_y, lam8


def _mix_kernel(um_ref, up_ref, un_ref, us_ref, yl_ref, yc_ref, x_ref, gate_ref, sh_ref, sc_ref,
                sd_ref, wg_ref, wo_ref, gp_ref, gf_ref, o_ref, h_ref, ext_ref, cat_ref, *,
                n_lat_tiles, seq, ctx):
    tt, b, pw_all = um_ref.shape
    pg = pw_all // len(POOL_WINDOWS)
    sw = us_ref.shape[-1]
    rows = tt * b
    i = pl.program_id(0)
    is_ctx = i >= n_lat_tiles
    t0 = jnp.where(is_ctx, i - n_lat_tiles, i) * tt
    seg = jnp.where(is_ctx, ctx, seq)
    has_prev = (t0 > 0).astype(F32)
    has_next = (t0 + tt < seg).astype(F32)
    hl = POOL_HALO
    ext_ref[0:hl] = up_ref[...] * has_prev
    ext_ref[hl:hl + tt] = um_ref[...]
    ext_ref[hl + tt:hl + tt + hl] = un_ref[...] * has_next

    tl = t0 + lax.broadcasted_iota(jnp.int32, (tt, b, pg), 0)
    for g, w in enumerate(POOL_WINDOWS):
        sl = slice(g * pg, (g + 1) * pg)
        lo = hl - w // 2
        acc = ext_ref[lo:lo + tt, :, sl]
        for k in range(1, w):
            acc = acc + ext_ref[lo + k:lo + k + tt, :, sl]
        cnt = (jnp.minimum(tl + w // 2, seg) - jnp.maximum(tl - w // 2, 0)).astype(F32)
        p = acc / cnt - ext_ref[hl:hl + tt, :, sl]
        cat_ref[:, sl] = p.reshape(rows, pg).astype(BF16)

    y = jnp.where(is_ctx, yc_ref[0] + yc_ref[1], yl_ref[0] + yl_ref[1])
    yv = y.reshape(rows, sw) + sd_ref[...] * us_ref[...].reshape(rows, sw)
    yv = jax.nn.gelu(yv)
    glu = jax.nn.sigmoid(jnp.dot(yv.astype(BF16), wg_ref[...], preferred_element_type=F32))
    cat_ref[:, pw_all:] = (yv * glu).astype(BF16)

    mix = jnp.dot(cat_ref[...], wo_ref[...], preferred_element_type=F32)
    x_new = x_ref[...] + gate_ref[...] * _rms(mix, gp_ref[...]).reshape(o_ref.shape)
    o_ref[...] = x_new
    h = _rms(x_new, gf_ref[...]) * (1.0 + sc_ref[...]) + sh_ref[...]
    h_ref[...] = h.reshape(h_ref.shape).astype(BF16)


def _mix(u, y_lat, y_ctx, xs, mod, ssm_d, w_glu, w_out, g_post, g_ffn, layer, tt, n_tiles,
         n_lat_tiles, seq, ctx):
    s, b, d = xs.shape
    mixw = u.shape[-1]
    sw = y_lat.shape[-1]
    pw_all = mixw - sw
    hl = POOL_HALO
    hpt = tt // hl
    last_halo = n_tiles * hpt - 1
    sel = lambda i: jnp.where(i >= n_lat_tiles, 1, 0)
    return pl.pallas_call(
        functools.partial(_mix_kernel, n_lat_tiles=n_lat_tiles, seq=seq, ctx=ctx),
        grid=(n_tiles,),
        in_specs=[
            pl.BlockSpec((tt, b, pw_all), lambda i: (i, 0, 0)),
            pl.BlockSpec((hl, b, pw_all), lambda i: (jnp.maximum(i * hpt - 1, 0), 0, 0)),
            pl.BlockSpec((hl, b, pw_all), lambda i: (jnp.minimum((i + 1) * hpt, last_halo), 0, 0)),
            pl.BlockSpec((tt, b, sw), lambda i: (i, 0, pw_all // sw)),
            pl.BlockSpec((2, tt, b, sw), lambda i: (0, jnp.minimum(i, n_lat_tiles - 1), 0, 0)),
            pl.BlockSpec((2, tt, b, sw), lambda i: (0, jnp.maximum(i - n_lat_tiles, 0), 0, 0)),
            pl.BlockSpec((tt, b, d), lambda i: (i, 0, 0)),
            pl.BlockSpec((None, None, b, d), lambda i: (layer, sel(i), 0, 2)),
            pl.BlockSpec((None, None, b, d), lambda i: (layer, sel(i), 0, 3)),
            pl.BlockSpec((None, None, b, d), lambda i: (layer, sel(i), 0, 4)),
            _const_spec((1, sw)),
            _const_spec(w_glu.shape[1:], layer),
            _const_spec(w_out.shape[1:], layer),
            _const_spec((1, d)),
            _const_spec((1, d)),
        ],
        out_specs=[pl.BlockSpec((tt, b, d), lambda i: (i, 0, 0)),
                   pl.BlockSpec((tt * b, d), lambda i: (i, 0))],
        out_shape=[jax.ShapeDtypeStruct(xs.shape, F32),
                   jax.ShapeDtypeStruct((n_tiles * tt * b, d), BF16)],
        scratch_shapes=[
            pltpu.VMEM((tt + 2 * hl, b, pw_all), F32),
            pltpu.VMEM((tt * b, mixw), BF16),
        ],
        input_output_aliases={6: 0},
        compiler_params=_params("arbitrary"),
        name="mix",
    )(u, u, u, u, y_lat, y_ctx, xs, mod, mod, mod, ssm_d, w_glu, w_out, g_post, g_ffn)


def _ffn_up_kernel(h_ref, w_ref, z_ref):
    z = jnp.dot(h_ref[...], w_ref[...], preferred_element_type=F32).astype(BF16)
    fb = z_ref.shape[-1]
    for k in range(z_ref.shape[0]):
        z_ref[k] = z[:, k * fb:(k + 1) * fb]


def _ffn_up(h, w_up, layer, tm, nb, fb):
    rows, d = h.shape
    n = w_up.shape[-1]
    return pl.pallas_call(
        _ffn_up_kernel,
        grid=(rows // tm, n // nb),
        in_specs=[
            pl.BlockSpec((tm, d), lambda i, j: (i, 0)),
            pl.BlockSpec((None, d, nb), lambda i, j: (layer, 0, j)),
        ],
        out_specs=pl.BlockSpec((nb // fb, tm, fb), lambda i, j: (j, i, 0)),
        out_shape=jax.ShapeDtypeStruct((n // fb, rows, fb), BF16),
        compiler_params=_params("arbitrary", "arbitrary"),
        name="ffn_up",
    )(h, w_up)


FFN_QUARTERS = 4
CONV_PIECE_STEPS = 4


def _ffn_down_kernel(va_ref, vm_ref, vb_ref, ga_ref, gm_ref, gb_ref, kv_ref, kg_ref, wd_ref, x_ref,
                     gate_ref, gp_ref, o_ref, *scratch, n_lat_tiles, n_ctx_tiles, batch_major_out):
    acc_ref = scratch[0] if batch_major_out else None
    (act_ref, kb_ref, eva_ref, evm0_ref, evm1_ref, evb_ref, ega_ref, egm0_ref, egm1_ref,
     egb_ref) = scratch[1:] if batch_major_out else scratch
    w, b, fb = evm0_ref.shape[0] - 2, evm0_ref.shape[1], evm0_ref.shape[2]
    i = pl.program_id(0)
    j = pl.program_id(1)
    nf = pl.num_programs(1) - 1
    is_ctx = i >= n_lat_tiles
    is_lat = jnp.logical_not(is_ctx)
    ic = i - n_lat_tiles
    lat_f = is_lat.astype(F32)
    ctx_f = is_ctx.astype(F32)
    vert_up = jnp.logical_and(is_lat, i > 0).astype(F32)
    vert_dn = jnp.logical_and(is_lat, i < n_lat_tiles - 1).astype(F32)
    hz_prev = jnp.logical_and(is_ctx, ic > 0).astype(F32)
    hz_next = jnp.logical_and(is_ctx, ic < n_ctx_tiles - 1).astype(F32)
    slot = j % 2

    @pl.when(j == 0)
    def _():
        if batch_major_out:
            acc_ref[...] = jnp.zeros(acc_ref.shape, F32)
        else:
            o_ref[...] = jnp.zeros(o_ref.shape, F32)
        act_ref[1] = jnp.zeros(act_ref.shape[1:], BF16)

    streams = ((va_ref, vm_ref, vb_ref, (eva_ref, evm0_ref, evm1_ref, evb_ref), kv_ref),
               (ga_ref, gm_ref, gb_ref, (ega_ref, egm0_ref, egm1_ref, egb_ref), kg_ref))
    pair = 2 * b
    zero_row = jnp.zeros((1, b, fb), F32)
    edge = lambda ref, row0, part: ref[row0:row0 + pair, :].astype(F32)[part * b:(part + 1) * b]
    for which, (a_ref, m_ref, b_ref, (ea_ref, em0_ref, em1_ref, eb_ref), k_ref) in enumerate(streams):
        ea_ref[0:1] = zero_row
        ea_ref[w + 1:w + 2] = zero_row
        eb_ref[0:1] = zero_row
        eb_ref[w + 1:w + 2] = zero_row
        em0_ref[0:1] = (edge(a_ref, (w - 2) * b, 1) * hz_prev).reshape(1, b, fb)
        em0_ref[w + 1:w + 2] = (edge(m_ref, w * b, 0) * ctx_f).reshape(1, b, fb)
        em1_ref[0:1] = (edge(m_ref, (w - 2) * b, 1) * ctx_f).reshape(1, b, fb)
        em1_ref[w + 1:w + 2] = (edge(b_ref, 0, 0) * hz_next).reshape(1, b, fb)
        for row, flags in enumerate(((vert_up, None, lat_f), (lat_f, None, vert_dn))):
            for r, flag in enumerate(flags):
                for dc in range(3):
                    kr = k_ref[3 * r + dc:3 * r + dc + 1, :]
                    kr = kr if flag is None else kr * flag
                    kb_ref[18 * which + 9 * row + 3 * r + dc] = jnp.broadcast_to(kr, (b, fb))

    tq = w // FFN_QUARTERS
    nq_lanes = x_ref.shape[-1] // FFN_QUARTERS
    tp = CONV_PIECE_STEPS

    def stage(q):
        lo, hi = q * tq * b, (q + 1) * tq * b
        for a_ref, m_ref, b_ref, (ea_ref, em0_ref, em1_ref, eb_ref), _ in streams:
            for src, off, dst in ((a_ref, 0, ea_ref), (m_ref, 0, em0_ref), (m_ref, w * b, em1_ref),
                                  (b_ref, 0, eb_ref)):
                blk = src[off + lo:off + hi, :].astype(F32)
                dst[1 + q * tq:1 + (q + 1) * tq] = blk.reshape(tq, b, fb)

    def conv_piece(row, t0, lanes, which):
        out = None
        for r, e_ref in enumerate(streams[which][3][row:row + 3]):
            for dc in range(3):
                tap = kb_ref[18 * which + 9 * row + 3 * r + dc, :, lanes]
                term = e_ref[t0 + dc:t0 + dc + tp, :, lanes] * tap
                out = term if out is None else out + term
        return out

    def project(q):
        cols = slice(q * nq_lanes, (q + 1) * nq_lanes)
        part = jnp.dot(act_ref[1 - slot], wd_ref[:, cols], preferred_element_type=F32)
        if batch_major_out:
            acc_ref[:, cols] += part
        else:
            o_ref[:, :, cols] += part.reshape(2 * w, b, nq_lanes)

    @pl.when(j < nf)
    def _():
        stage(0)
        for q in range(FFN_QUARTERS):
            if q + 1 < FFN_QUARTERS:
                stage(q + 1)
            project(q)
            for row in range(2):
                for lc in range(fb // LANES):
                    lanes = slice(lc * LANES, (lc + 1) * LANES)
                    for t0 in range(q * tq, (q + 1) * tq, tp):
                        val = conv_piece(row, t0, lanes, 0)
                        gat = conv_piece(row, t0, lanes, 1)
                        act = (val * jax.nn.silu(gat)).reshape(tp * b, LANES).astype(BF16)
                        act_ref[slot, (row * w + t0) * b:(row * w + t0 + tp) * b, lanes] = act

    @pl.when(j == nf)
    def _():
        for q in range(FFN_QUARTERS):
            project(q)
        if batch_major_out:
            out = x_ref[...] + gate_ref[...] * _rms(acc_ref[...], gp_ref[...]).reshape(x_ref.shape)
            for t in range(out.shape[0]):
                o_ref[:, t, :] = out[t]
        else:
            o_ref[...] = x_ref[...] + gate_ref[...] * _rms(o_ref[...], gp_ref[...])


def _ffn_down(z, w_conv9, w_down, xs, mod, g_post, layer, fb, n_tiles, n_lat_tiles, n_ctx_tiles,
              batch_major_out):
    s, b, d = xs.shape
    f = w_down.shape[1]
    nf = f // fb
    tt = 2 * GRID_W
    rows = tt * b
    half = rows // 2
    up = lambda i: jnp.maximum(2 * i - 1, 0)
    dn = lambda i: jnp.minimum(2 * i + 2, 2 * n_tiles - 1)
    sel = lambda i: jnp.where(i >= n_lat_tiles, 1, 0)
    cur = lambda j: jnp.minimum(j, nf - 1)
    prv = lambda j: jnp.maximum(j - 1, 0)
    zspec = lambda n_rows, row, col0: pl.BlockSpec((None, n_rows, fb),
                                                   lambda i, j: (col0 + cur(j), row(i), 0))
    same = lambda i: i
    assert GRID_W % (FFN_QUARTERS * SUBLANES) == 0 and d % (FFN_QUARTERS * LANES) == 0
    return pl.pallas_call(
        functools.partial(_ffn_down_kernel, n_lat_tiles=n_lat_tiles, n_ctx_tiles=n_ctx_tiles,
                          batch_major_out=batch_major_out),
        grid=(n_tiles, nf + 1),
        in_specs=[
            zspec(half, up, 0), zspec(rows, same, 0), zspec(half, dn, 0),
            zspec(half, up, nf), zspec(rows, same, nf), zspec(half, dn, nf),
            pl.BlockSpec((9, fb), lambda i, j: (0, cur(j))),
            pl.BlockSpec((9, fb), lambda i, j: (0, nf + cur(j))),
            pl.BlockSpec((None, fb, d), lambda i, j: (layer, prv(j), 0)),
            pl.BlockSpec((tt, b, d), lambda i, j: (i, 0, 0)),
            pl.BlockSpec((None, None, b, d), lambda i, j: (layer, sel(i), 0, 5)),
            _const_spec((1, d)),
        ],
        out_specs=(pl.BlockSpec((b, tt, d), lambda i, j: (0, i, 0)) if batch_major_out
                   else pl.BlockSpec((tt, b, d), lambda i, j: (i, 0, 0))),
        out_shape=jax.ShapeDtypeStruct((b, n_tiles * tt, d) if batch_major_out else xs.shape, F32),
        scratch_shapes=([pltpu.VMEM((rows, d), F32)] if batch_major_out else []) + [
            pltpu.VMEM((2, rows, fb), BF16),
            pltpu.VMEM((36, b, fb), F32),
        ] + [pltpu.VMEM((GRID_W + 2, b, fb), F32)] * 8,
        input_output_aliases={} if batch_major_out else {9: 0},
        compiler_params=_params("arbitrary", "arbitrary"),
        name="ffn_down",
    )(z, z, z, z, z, z, w_conv9, w_conv9, w_down, xs, mod, g_post)


def _pick(limit, total):
    t = min(limit, total)
    while total % t:
        t -= 1
    return t


def kernel(x, c, ctx, c_ctx, w_ada, b_ada, w_in, w_pool, pool_scale, ssm_a_re, ssm_a_im, ssm_log_dt,
           ssm_b_re, ssm_b_im, ssm_c_re, ssm_c_im, ssm_d, w_glu, w_out, g_pre_mix, g_post_mix,
           g_pre_ffn, g_post_ffn, w_up, w_conv, w_down):
    bsz, seq, d = x.shape
    n_ctx = ctx.shape[1]
    depth = w_ada.shape[0]
    mixw = w_in.shape[-1]
    sw = ssm_d.shape[-1]
    f = w_down.shape[1]
    assert bsz == SUBLANES and seq % GRID_W == 0 and n_ctx % GRID_W == 0
    assert sw % LANES == 0 and (mixw - sw) % sw == 0 and LANES % ssm_b_re.shape[-1] == 0
    s_all = seq + n_ctx
    tt = GRID_W
    n_lat_tiles = seq // tt
    n_ctx_tiles = n_ctx // tt
    n_tiles = n_lat_tiles + n_ctx_tiles
    tt_up = 2 * GRID_W
    assert seq % tt_up == 0 and n_ctx % tt_up == 0
    nb_up = _pick(1024, 2 * f)
    fb_in_place = _pick(512, f)
    fb_fresh = _pick(256, f)
    lat_chunks = _pick(128, seq // SSM_CHUNK)
    ctx_chunks = _pick(64, n_ctx // SSM_CHUNK)
    assert (seq // SSM_CHUNK) % ctx_chunks == 0

    c16 = jnp.concatenate([c, jnp.broadcast_to(c_ctx[None, :], (SUBLANES, d))], axis=0)
    mod_all = _ada_all(c16, w_ada, b_ada, _pick(1024, w_ada.shape[-1]))

    row = lambda v: v.reshape(1, -1).astype(F32)
    n_pool = len(POOL_WINDOWS)
    pw_all = mixw - sw
    pg = pw_all // n_pool
    w_pool_scaled = w_pool.astype(F32) * pool_scale.astype(F32).reshape(depth, n_pool, 1, pg)
    w_out_pool = jnp.einsum('lgio,lgod->lgid', w_pool_scaled,
                            w_out[:, :pw_all].astype(F32).reshape(depth, n_pool, pg, d),
                            precision=lax.Precision.HIGHEST).reshape(depth, pw_all, d)
    w_out = jnp.concatenate([w_out_pool, w_out[:, pw_all:].astype(F32)], axis=1)
    w_in, w_glu, w_out, w_up, w_down = (v.astype(BF16) for v in (w_in, w_glu, w_out, w_up, w_down))
    ssm_params = (ssm_a_re, ssm_a_im, ssm_log_dt, ssm_b_re, ssm_b_im, ssm_c_re, ssm_c_im)
    mats = jax.vmap(jax.vmap(_ssm_matrices))(*ssm_params)
    nbk = sw // LANES
    lane0 = (mixw - sw) // LANES
    h0 = jnp.zeros((2, nbk, bsz, mats[0].shape[-1]), F32)
    for l in range(depth):
        last = l == depth - 1
        if l == 0:
            xs, u = _premix_first(x, ctx, mod_all, row(g_pre_mix[l]), w_in, l, tt)
        else:
            u = _premix(xs, mod_all, row(g_pre_mix[l]), w_in, l, tt, n_lat_tiles)
        u5 = u.reshape(s_all // SSM_CHUNK, SSM_CHUNK, bsz, mixw)
        y_ctx, h_ctx = _ssm(u5, mats, l, h0, ctx_chunks, (seq // SSM_CHUNK) // ctx_chunks,
                            (n_ctx // SSM_CHUNK) // ctx_chunks, lane0)
        y_lat, _ = _ssm(u5, mats, l, h_ctx, lat_chunks, 0, (seq // SSM_CHUNK) // lat_chunks, lane0)

        nt = n_lat_tiles if last else n_tiles
        xs, h_ffn = _mix(u, y_lat.reshape(2, seq, bsz, sw), y_ctx.reshape(2, n_ctx, bsz, sw), xs,
                         mod_all, row(ssm_d[l]), w_glu, w_out, row(g_post_mix[l]), row(g_pre_ffn[l]), l,
                         tt, nt, n_lat_tiles, seq, n_ctx)
        nt_up = (seq if last else s_all) // tt_up
        fb = fb_fresh if last else fb_in_place
        z = _ffn_up(h_ffn, w_up, l, _pick(2048, math.gcd(seq, n_ctx) * bsz), nb_up, fb)
        xs = _ffn_down(z, w_conv[l].reshape(9, 2 * f).astype(F32), w_down, xs, mod_all,
                       row(g_post_ffn[l]), l, fb, nt_up, seq // tt_up, n_ctx // tt_up, last)
    return xs
```

```python
import functools
import math

import jax
import jax.numpy as jnp
from jax import lax
from jax.experimental import pallas as pl
from jax.experimental.pallas import tpu as pltpu

GRID_W = 64
POOL_WINDOWS = (2, 4, 8, 16)
POOL_HALO = max(POOL_WINDOWS) // 2
EPS = 1e-6
SUBLANES = 8
LANES = 128
SSM_CHUNK = 8
VMEM_LIMIT_BYTES = 62 * 1024 * 1024

F32 = jnp.float32
BF16 = jnp.bfloat16


def _params(*sem):
    return pltpu.CompilerParams(dimension_semantics=sem, vmem_limit_bytes=VMEM_LIMIT_BYTES)


def _const_spec(shape, layer=None):
    zeros = (0,) * len(shape)
    if layer is None:
        return pl.BlockSpec(shape, lambda *_: zeros, pipeline_mode=pl.Buffered(1))
    return pl.BlockSpec((None,) + tuple(shape), lambda *_: (layer,) + zeros, pipeline_mode=pl.Buffered(1))


def _rms(v, gain):
    ms = jnp.mean(v * v, axis=-1, keepdims=True)
    return v * lax.rsqrt(ms + EPS) * gain


def _ada_kernel(c_ref, w_ref, b_ref, o_ref):
    s = jax.nn.silu(c_ref[...]).astype(BF16)
    r = jnp.dot(s, w_ref[...].astype(BF16), preferred_element_type=F32) + b_ref[...]
    o_ref[...] = r.reshape(o_ref.shape)


def _ada_all(c16, w_ada, b_ada, nb):
    depth, d, n = w_ada.shape
    return pl.pallas_call(
        _ada_kernel,
        grid=(depth, n // nb),
        in_specs=[
            pl.BlockSpec((2 * SUBLANES, d), lambda l, j: (0, 0)),
            pl.BlockSpec((None, d, nb), lambda l, j: (l, 0, j)),
            pl.BlockSpec((None, 1, nb), lambda l, j: (l, 0, j)),
        ],
        out_specs=pl.BlockSpec((None, 2, SUBLANES, nb), lambda l, j: (l, 0, 0, j)),
        out_shape=jax.ShapeDtypeStruct((depth, 2, SUBLANES, n), F32),
        compiler_params=_params("arbitrary", "arbitrary"),
        name="ada",
    )(c16, w_ada, b_ada.reshape(depth, 1, n))


def _premix_kernel(x_ref, sh_ref, sc_ref, g_ref, w_ref, u_ref):
    tt, b, d = x_ref.shape
    h = _rms(x_ref[...], g_ref[...]) * (1.0 + sc_ref[...]) + sh_ref[...]
    u = jnp.dot(h.reshape(tt * b, d).astype(BF16), w_ref[...], preferred_element_type=F32)
    u_ref[...] = u.reshape(u_ref.shape)


def _premix_first_kernel(xl_ref, xc_ref, sh_ref, sc_ref, g_ref, w_ref, xs_ref, u_ref, *, n_lat_tiles):
    tt = xs_ref.shape[0]
    is_ctx = pl.program_id(0) >= n_lat_tiles

    @pl.when(is_ctx)
    def _():
        for t in range(tt):
            xs_ref[t] = xc_ref[:, t, :]

    @pl.when(jnp.logical_not(is_ctx))
    def _():
        for t in range(tt):
            xs_ref[t] = xl_ref[:, t, :]

    _premix_kernel(xs_ref, sh_ref, sc_ref, g_ref, w_ref, u_ref)


def _premix_first(x, ctx, mod, g, w, layer, tt):
    b, seq, d = x.shape
    n_ctx = ctx.shape[1]
    n = w.shape[-1]
    n_lat_tiles = seq // tt
    sel = lambda i: jnp.where(i >= n_lat_tiles, 1, 0)
    stream = jax.ShapeDtypeStruct((seq + n_ctx, b, d), F32)
    return pl.pallas_call(
        functools.partial(_premix_first_kernel, n_lat_tiles=n_lat_tiles),
        grid=((seq + n_ctx) // tt,),
        in_specs=[
            pl.BlockSpec((b, tt, d), lambda i: (0, jnp.minimum(i, n_lat_tiles - 1), 0)),
            pl.BlockSpec((b, tt, d), lambda i: (0, jnp.maximum(i - n_lat_tiles, 0), 0)),
            pl.BlockSpec((None, None, b, d), lambda i: (layer, sel(i), 0, 0)),
            pl.BlockSpec((None, None, b, d), lambda i: (layer, sel(i), 0, 1)),
            _const_spec((1, d)),
            _const_spec((d, n), layer),
        ],
        out_specs=[pl.BlockSpec((tt, b, d), lambda i: (i, 0, 0)), pl.BlockSpec((tt, b, n), lambda i: (i, 0, 0))],
        out_shape=[stream, jax.ShapeDtypeStruct((seq + n_ctx, b, n), F32)],
        compiler_params=_params("arbitrary"),
        name="premix_first",
    )(x, ctx, mod, mod, g, w)


def _premix(xs, mod, g, w, layer, tt, n_lat_tiles):
    s, b, d = xs.shape
    n = w.shape[-1]
    sel = lambda i: jnp.where(i >= n_lat_tiles, 1, 0)
    return pl.pallas_call(
        _premix_kernel,
        grid=(s // tt,),
        in_specs=[
            pl.BlockSpec((tt, b, d), lambda i: (i, 0, 0)),
            pl.BlockSpec((None, None, b, d), lambda i: (layer, sel(i), 0, 0)),
            pl.BlockSpec((None, None, b, d), lambda i: (layer, sel(i), 0, 1)),
            _const_spec((1, d)),
            _const_spec((d, n), layer),
        ],
        out_specs=pl.BlockSpec((tt, b, n), lambda i: (i, 0, 0)),
        out_shape=jax.ShapeDtypeStruct((s, b, n), F32),
        compiler_params=_params("arbitrary"),
        name="premix",
    )(xs, mod, mod, g, w)


def _ssm_kernel(u_ref, mo_ref, my_ref, lam_ref, h0_ref, y_ref, hf_ref, s_ref, hin_ref, hc_ref, *,
                chunks):
    c_n = chunks
    sb = hc_ref.shape[-1]
    half = sb // 2
    d = pl.program_id(0)
    t = pl.program_id(2)

    @pl.when(t == 0)
    def _():
        hc_ref[...] = h0_ref[...]

    step_of = lambda k: jnp.where(d == 0, k, SSM_CHUNK - 1 - k)
    u_rows = jnp.concatenate(
        [u_ref[:, pl.ds(step_of(k), 1)].reshape(c_n * SUBLANES, LANES).astype(BF16)
         for k in range(SSM_CHUNK)], axis=-1)
    s_ref[...] = jnp.dot(u_rows, mo_ref[...], preferred_element_type=F32).reshape(s_ref.shape)

    lre = jnp.broadcast_to(lam_ref[0:1, :], (SUBLANES, half))
    lim = jnp.broadcast_to(lam_ref[1:2, :], (SUBLANES, half))

    def step(i, carry):
        hr, hi = carry
        c = jnp.where(d == 0, i, c_n - 1 - i)
        hin_ref[c, :, :half] = hr
        hin_ref[c, :, half:] = hi
        sr = s_ref[c, :, :half]
        si = s_ref[c, :, half:]
        return lre * hr - lim * hi + sr, lre * hi + lim * hr + si

    hr, hi = lax.fori_loop(0, c_n, step, (hc_ref[:, :half], hc_ref[:, half:]))
    hc_ref[:, :half] = hr
    hc_ref[:, half:] = hi
    hf_ref[:, :half] = hr
    hf_ref[:, half:] = hi

    h_rows = hin_ref[...].reshape(c_n * SUBLANES, sb).astype(BF16)
    kk = SSM_CHUNK * LANES
    for p in range(SSM_CHUNK // 2):
        cols = slice(2 * p * LANES, (2 * p + 2) * LANES)
        seen = (2 * p + 2) * LANES
        y = (jnp.dot(u_rows[:, :seen], my_ref[0:seen, cols], preferred_element_type=F32)
             + jnp.dot(h_rows, my_ref[kk:, cols], preferred_element_type=F32))
        for k in (2 * p, 2 * p + 1):
            y_ref[:, pl.ds(step_of(k), 1)] = (
                y[:, (k - 2 * p) * LANES:(k - 2 * p + 1) * LANES].reshape(c_n, 1, SUBLANES, LANES))


def _ssm(u5, mats, layer, h0, chunks, block_off, n_tiles, ssm_lane_block0):
    m_out, m_y, lam8 = mats
    b = u5.shape[2]
    _, _, nb, kk, sb = m_out.shape
    tile = lambda d, t: jnp.where(d == 0, t, n_tiles - 1 - t)
    y_shape = (2, n_tiles * chunks, SSM_CHUNK, b, nb * LANES)
    return pl.pallas_call(
        functools.partial(_ssm_kernel, chunks=chunks),
        grid=(2, nb, n_tiles),
        in_specs=[
            pl.BlockSpec((chunks, SSM_CHUNK, b, LANES),
                         lambda d, j, t: (tile(d, t) + block_off, 0, 0, ssm_lane_block0 + j)),
            pl.BlockSpec((None, None, None, kk, sb), lambda d, j, t: (layer, d, j, 0, 0)),
            pl.BlockSpec((None, None, None, kk + sb, kk), lambda d, j, t: (layer, d, j, 0, 0)),
            pl.BlockSpec((None, None, None, 2, sb // 2), lambda d, j, t: (layer, d, j, 0, 0)),
            pl.BlockSpec((None, None, b, sb), lambda d, j, t: (d, j, 0, 0)),
        ],
        out_specs=[
            pl.BlockSpec((None, chunks, SSM_CHUNK, b, LANES), lambda d, j, t: (d, tile(d, t), 0, 0, j)),
            pl.BlockSpec((None, None, b, sb), lambda d, j, t: (d, j, 0, 0)),
        ],
        out_shape=[jax.ShapeDtypeStruct(y_shape, F32), jax.ShapeDtypeStruct(h0.shape, F32)],
        scratch_shapes=[
            pltpu.VMEM((chunks, b, sb), F32),
            pltpu.VMEM((chunks, b, sb), F32),
            pltpu.VMEM((b, sb), F32),
        ],
        compiler_params=_params("arbitrary", "arbitrary", "arbitrary"),
        name="ssm",
    )(u5, m_out, m_y, lam8, h0)


def _ssm_matrices(a_re, a_im, log_dt, b_re, b_im, c_re, c_im):
    g, p = a_re.shape
    h = b_re.shape[-1]
    t_n = SSM_CHUNK
    gpb = LANES // h
    nb = g // gpb
    a_re = a_re.astype(F32)
    a_im = a_im.astype(F32)
    dt = jnp.exp(log_dt.astype(F32))[:, None]
    mag = jnp.exp(a_re * dt)
    lam_re = mag * jnp.cos(a_im * dt)
    lam_im = mag * jnp.sin(a_im * dt)
    denom = a_re * a_re + a_im * a_im
    nr, ni = lam_re - 1.0, lam_im
    f_re = ((nr * a_re + ni * a_im) / denom)[..., None]
    f_im = ((ni * a_re - nr * a_im) / denom)[..., None]
    b_re = b_re.astype(F32)
    b_im = b_im.astype(F32)
    bb_re = f_re * b_re - f_im * b_im
    bb_im = f_re * b_im + f_im * b_re
    c_re = c_re.astype(F32)
    c_im = c_im.astype(F32)
    pr, pi = [jnp.ones_like(lam_re)], [jnp.zeros_like(lam_im)]
    for _ in range(t_n):
        pr, pi = (pr + [pr[-1] * lam_re - pi[-1] * lam_im], pi + [pr[-1] * lam_im + pi[-1] * lam_re])
    pr = jnp.stack(pr)
    pi = jnp.stack(pi)

    q_re = pr[:t_n, :, :, None] * bb_re - pi[:t_n, :, :, None] * bb_im
    q_im = pr[:t_n, :, :, None] * bb_im + pi[:t_n, :, :, None] * bb_re
    hi = lax.Precision.HIGHEST
    taps = (jnp.einsum('gop,kgpi->kgoi', c_re, q_re, precision=hi)
            - jnp.einsum('gop,kgpi->kgoi', c_im, q_im, precision=hi))
    lag = jnp.arange(t_n)[None, :] - jnp.arange(t_n)[:, None]
    toep = jnp.where((lag >= 0)[:, :, None, None, None], taps[jnp.clip(lag, 0, t_n - 1)], 0.0)
    toep = toep.reshape(t_n, t_n, nb, gpb, h, h)
    cr, ci = c_re[None], c_im[None]
    lr, li = pr[1:, :, None, :], pi[1:, :, None, :]
    w_in = jnp.stack([cr * lr - ci * li, -(cr * li + ci * lr)])
    w_in = w_in.reshape(2, t_n, nb, gpb, h, p)
    w_out = jnp.stack([q_re[::-1], q_im[::-1]]).reshape(2, t_n, nb, gpb, p, h)
    k_dim = t_n * gpb * h
    sb = 2 * gpb * p

    def on_group_diagonal(compact, row_width, col_width, n_cols):
        n_rows, n_compact = compact.shape[1:]
        col = jnp.arange(n_cols)
        src = (col // (gpb * col_width)) * col_width + col % col_width
        place = (jnp.arange(n_compact)[:, None] == src[None, :]).astype(F32)
        full = jnp.einsum('jrk,kc->jrc', compact, place)
        row_group = (jnp.arange(n_rows) // row_width) % gpb
        col_group = (col // col_width) % gpb
        return jnp.where(row_group[:, None] == col_group[None, :], full, 0.0).astype(BF16)

    m_toep = on_group_diagonal(toep.transpose(2, 0, 3, 5, 1, 4).reshape(nb, k_dim, t_n * h), h, h, k_dim)
    m_in = on_group_diagonal(w_in.transpose(2, 0, 3, 5, 1, 4).reshape(nb, sb, t_n * h), p, h, k_dim)
    m_out = on_group_diagonal(w_out.transpose(2, 1, 3, 5, 0, 4).reshape(nb, k_dim, 2 * p), h, p, sb)
    m_y = jnp.concatenate([m_toep, m_in], axis=1)
    lam8 = jnp.stack([pr[t_n].reshape(nb, gpb * p), pi[t_n].reshape(nb, gpb * p)], axis=1)
    return m_out, m_y, lam8


def _mix_kernel(um_ref, up_ref, un_ref, us_ref, yl_ref, yc_ref, x_ref, gate_ref, sh_ref, sc_ref,
                sd_ref, wg_ref, wo_ref, gp_ref, gf_ref, o_ref, h_ref, ext_ref, cat_ref, *,
                n_lat_tiles, seq, ctx):
    tt, b, pw_all = um_ref.shape
    pg = pw_all // len(POOL_WINDOWS)
    sw = us_ref.shape[-1]
    rows = tt * b
    i = pl.program_id(0)
    is_ctx = i >= n_lat_tiles
    t0 = jnp.where(is_ctx, i - n_lat_tiles, i) * tt
    seg = jnp.where(is_ctx, ctx, seq)
    has_prev = (t0 > 0).astype(F32)
    has_next = (t0 + tt < seg).astype(F32)
    hl = POOL_HALO
    ext_ref[0:hl] = up_ref[...] * has_prev
    ext_ref[hl:hl + tt] = um_ref[...]
    ext_ref[hl + tt:hl + tt + hl] = un_ref[...] * has_next

    tl = t0 + lax.broadcasted_iota(jnp.int32, (tt, b, pg), 0)
    for g, w in enumerate(POOL_WINDOWS):
        sl = slice(g * pg, (g + 1) * pg)
        lo = hl - w // 2
        acc = ext_ref[lo:lo + tt, :, sl]
        for k in range(1, w):
            acc = acc + ext_ref[lo + k:lo + k + tt, :, sl]
        cnt = (jnp.minimum(tl + w // 2, seg) - jnp.maximum(tl - w // 2, 0)).astype(F32)
        p = acc / cnt - ext_ref[hl:hl + tt, :, sl]
        cat_ref[:, sl] = p.reshape(rows, pg).astype(BF16)

    y = jnp.where(is_ctx, yc_ref[0] + yc_ref[1], yl_ref[0] + yl_ref[1])
    yv = y.reshape(rows, sw) + sd_ref[...] * us_ref[...].reshape(rows, sw)
    yv = jax.nn.gelu(yv)
    glu = jax.nn.sigmoid(jnp.dot(yv.astype(BF16), wg_ref[...], preferred_element_type=F32))
    cat_ref[:, pw_all:] = (yv * glu).astype(BF16)

    mix = jnp.dot(cat_ref[...], wo_ref[...], preferred_element_type=F32)
    x_new = x_ref[...] + gate_ref[...] * _rms(mix, gp_ref[...]).reshape(o_ref.shape)
    o_ref[...] = x_new
    h = _rms(x_new, gf_ref[...]) * (1.0 + sc_ref[...]) + sh_ref[...]
    h_ref[...] = h.reshape(h_ref.shape).astype(BF16)


def _mix(u, y_lat, y_ctx, xs, mod, ssm_d, w_glu, w_out, g_post, g_ffn, layer, tt, n_tiles,
         n_lat_tiles, seq, ctx):
    s, b, d = xs.shape
    mixw = u.shape[-1]
    sw = y_lat.shape[-1]
    pw_all = mixw - sw
    hl = POOL_HALO
    hpt = tt // hl
    last_halo = n_tiles * hpt - 1
    sel = lambda i: jnp.where(i >= n_lat_tiles, 1, 0)
    return pl.pallas_call(
        functools.partial(_mix_kernel, n_lat_tiles=n_lat_tiles, seq=seq, ctx=ctx),
        grid=(n_tiles,),
        in_specs=[
            pl.BlockSpec((tt, b, pw_all), lambda i: (i, 0, 0)),
            pl.BlockSpec((hl, b, pw_all), lambda i: (jnp.maximum(i * hpt - 1, 0), 0, 0)),
            pl.BlockSpec((hl, b, pw_all), lambda i: (jnp.minimum((i + 1) * hpt, last_halo), 0, 0)),
            pl.BlockSpec((tt, b, sw), lambda i: (i, 0, pw_all // sw)),
            pl.BlockSpec((2, tt, b, sw), lambda i: (0, jnp.minimum(i, n_lat_tiles - 1), 0, 0)),
            pl.BlockSpec((2, tt, b, sw), lambda i: (0, jnp.maximum(i - n_lat_tiles, 0), 0, 0)),
            pl.BlockSpec((tt, b, d), lambda i: (i, 0, 0)),
            pl.BlockSpec((None, None, b, d), lambda i: (layer, sel(i), 0, 2)),
            pl.BlockSpec((None, None, b, d), lambda i: (layer, sel(i), 0, 3)),
            pl.BlockSpec((None, None, b, d), lambda i: (layer, sel(i), 0, 4)),
            _const_spec((1, sw)),
            _const_spec(w_glu.shape[1:], layer),
            _const_spec(w_out.shape[1:], layer),
            _const_spec((1, d)),
            _const_spec((1, d)),
        ],
        out_specs=[pl.BlockSpec((tt, b, d), lambda i: (i, 0, 0)),
                   pl.BlockSpec((tt * b, d), lambda i: (i, 0))],
        out_shape=[jax.ShapeDtypeStruct(xs.shape, F32),
                   jax.ShapeDtypeStruct((n_tiles * tt * b, d), BF16)],
        scratch_shapes=[
            pltpu.VMEM((tt + 2 * hl, b, pw_all), F32),
            pltpu.VMEM((tt * b, mixw), BF16),
        ],
        input_output_aliases={6: 0},
        compiler_params=_params("arbitrary"),
        name="mix",
    )(u, u, u, u, y_lat, y_ctx, xs, mod, mod, mod, ssm_d, w_glu, w_out, g_post, g_ffn)


def _ffn_up_kernel(h_ref, w_ref, z_ref):
    z = jnp.dot(h_ref[...], w_ref[...], preferred_element_type=F32).astype(BF16)
    fb = z_ref.shape[-1]
    for k in range(z_ref.shape[0]):
        z_ref[k] = z[:, k * fb:(k + 1) * fb]


def _ffn_up(h, w_up, layer, tm, nb, fb):
    rows, d = h.shape
    n = w_up.shape[-1]
    return pl.pallas_call(
        _ffn_up_kernel,
        grid=(rows // tm, n // nb),
        in_specs=[
            pl.BlockSpec((tm, d), lambda i, j: (i, 0)),
            pl.BlockSpec((None, d, nb), lambda i, j: (layer, 0, j)),
        ],
        out_specs=pl.BlockSpec((nb // fb, tm, fb), lambda i, j: (j, i, 0)),
        out_shape=jax.ShapeDtypeStruct((n // fb, rows, fb), BF16),
        compiler_params=_params("arbitrary", "arbitrary"),
        name="ffn_up",
    )(h, w_up)


FFN_QUARTERS = 4
CONV_PIECE_STEPS = 4


def _ffn_down_kernel(va_ref, vm_ref, vb_ref, ga_ref, gm_ref, gb_ref, kv_ref, kg_ref, wd_ref, x_ref,
                     gate_ref, gp_ref, o_ref, *scratch, n_lat_tiles, n_ctx_tiles, batch_major_out):
    acc_ref = scratch[0] if batch_major_out else None
    (act_ref, kb_ref, eva_ref, evm0_ref, evm1_ref, evb_ref, ega_ref, egm0_ref, egm1_ref,
     egb_ref) = scratch[1:] if batch_major_out else scratch
    w, b, fb = evm0_ref.shape[0] - 2, evm0_ref.shape[1], evm0_ref.shape[2]
    i = pl.program_id(0)
    j = pl.program_id(1)
    nf = pl.num_programs(1) - 1
    is_ctx = i >= n_lat_tiles
    is_lat = jnp.logical_not(is_ctx)
    ic = i - n_lat_tiles
    lat_f = is_lat.astype(F32)
    ctx_f = is_ctx.astype(F32)
    vert_up = jnp.logical_and(is_lat, i > 0).astype(F32)
    vert_dn = jnp.logical_and(is_lat, i < n_lat_tiles - 1).astype(F32)
    hz_prev = jnp.logical_and(is_ctx, ic > 0).astype(F32)
    hz_next = jnp.logical_and(is_ctx, ic < n_ctx_tiles - 1).astype(F32)
    slot = j % 2

    @pl.when(j == 0)
    def _():
        if batch_major_out:
            acc_ref[...] = jnp.zeros(acc_ref.shape, F32)
        else:
            o_ref[...] = jnp.zeros(o_ref.shape, F32)
        act_ref[1] = jnp.zeros(act_ref.shape[1:], BF16)

    streams = ((va_ref, vm_ref, vb_ref, (eva_ref, evm0_ref, evm1_ref, evb_ref), kv_ref),
               (ga_ref, gm_ref, gb_ref, (ega_ref, egm0_ref, egm1_ref, egb_ref), kg_ref))
    pair = 2 * b
    zero_row = jnp.zeros((1, b, fb), F32)
    edge = lambda ref, row0, part: ref[row0:row0 + pair, :].astype(F32)[part * b:(part + 1) * b]
    for which, (a_ref, m_ref, b_ref, (ea_ref, em0_ref, em1_ref, eb_ref), k_ref) in enumerate(streams):
        ea_ref[0:1] = zero_row
        ea_ref[w + 1:w + 2] = zero_row
        eb_ref[0:1] = zero_row
        eb_ref[w + 1:w + 2] = zero_row
        em0_ref[0:1] = (edge(a_ref, (w - 2) * b, 1) * hz_prev).reshape(1, b, fb)
        em0_ref[w + 1:w + 2] = (edge(m_ref, w * b, 0) * ctx_f).reshape(1, b, fb)
        em1_ref[0:1] = (edge(m_ref, (w - 2) * b, 1) * ctx_f).reshape(1, b, fb)
        em1_ref[w + 1:w + 2] = (edge(b_ref, 0, 0) * hz_next).reshape(1, b, fb)
        for row, flags in enumerate(((vert_up, None, lat_f), (lat_f, None, vert_dn))):
            for r, flag in enumerate(flags):
                for dc in range(3):
                    kr = k_ref[3 * r + dc:3 * r + dc + 1, :]
                    kr = kr if flag is None else kr * flag
                    kb_ref[18 * which + 9 * row + 3 * r + dc] = jnp.broadcast_to(kr, (b, fb))

    tq = w // FFN_QUARTERS
    nq_lanes = x_ref.shape[-1] // FFN_QUARTERS
    tp = CONV_PIECE_STEPS

    def stage(q):
        lo, hi = q * tq * b, (q + 1) * tq * b
        for a_ref, m_ref, b_ref, (ea_ref, em0_ref, em1_ref, eb_ref), _ in streams:
            for src, off, dst in ((a_ref, 0, ea_ref), (m_ref, 0, em0_ref), (m_ref, w * b, em1_ref),
                                  (b_ref, 0, eb_ref)):
                blk = src[off + lo:off + hi, :].astype(F32)
                dst[1 + q * tq:1 + (q + 1) * tq] = blk.reshape(tq, b, fb)

    def conv_piece(row, t0, lanes, which):
        out = None
        for r, e_ref in enumerate(streams[which][3][row:row + 3]):
            for dc in range(3):
                tap = kb_ref[18 * which + 9 * row + 3 * r + dc, :, lanes]
                term = e_ref[t0 + dc:t0 + dc + tp, :, lanes] * tap
                out = term if out is None else out + term
        return out

    def project(q):
        cols = slice(q * nq_lanes, (q + 1) * nq_lanes)
        part = jnp.dot(act_ref[1 - slot], wd_ref[:, cols], preferred_element_type=F32)
        if batch_major_out:
            acc_ref[:, cols] += part
        else:
            o_ref[:, :, cols] += part.reshape(2 * w, b, nq_lanes)

    @pl.when(j < nf)
    def _():
        stage(0)
        for q in range(FFN_QUARTERS):
            if q + 1 < FFN_QUARTERS:
                stage(q + 1)
            project(q)
            for row in range(2):
                for lc in range(fb // LANES):
                    lanes = slice(lc * LANES, (lc + 1) * LANES)
                    for t0 in range(q * tq, (q + 1) * tq, tp):
                        val = conv_piece(row, t0, lanes, 0)
                        gat = conv_piece(row, t0, lanes, 1)
                        act = (val * jax.nn.silu(gat)).reshape(tp * b, LANES).astype(BF16)
                        act_ref[slot, (row * w + t0) * b:(row * w + t0 + tp) * b, lanes] = act

    @pl.when(j == nf)
    def _():
        for q in range(FFN_QUARTERS):
            project(q)
        if batch_major_out:
            out = x_ref[...] + gate_ref[...] * _rms(acc_ref[...], gp_ref[...]).reshape(x_ref.shape)
            for t in range(out.shape[0]):
                o_ref[:, t, :] = out[t]
        else:
            o_ref[...] = x_ref[...] + gate_ref[...] * _rms(o_ref[...], gp_ref[...])


def _ffn_down(z, w_conv9, w_down, xs, mod, g_post, layer, fb, n_tiles, n_lat_tiles, n_ctx_tiles,
              batch_major_out):
    s, b, d = xs.shape
    f = w_down.shape[1]
    nf = f // fb
    tt = 2 * GRID_W
    rows = tt * b
    half = rows // 2
    up = lambda i: jnp.maximum(2 * i - 1, 0)
    dn = lambda i: jnp.minimum(2 * i + 2, 2 * n_tiles - 1)
    sel = lambda i: jnp.where(i >= n_lat_tiles, 1, 0)
    cur = lambda j: jnp.minimum(j, nf - 1)
    prv = lambda j: jnp.maximum(j - 1, 0)
    zspec = lambda n_rows, row, col0: pl.BlockSpec((None, n_rows, fb),
                                                   lambda i, j: (col0 + cur(j), row(i), 0))
    same = lambda i: i
    assert GRID_W % (FFN_QUARTERS * SUBLANES) == 0 and d % (FFN_QUARTERS * LANES) == 0
    return pl.pallas_call(
        functools.partial(_ffn_down_kernel, n_lat_tiles=n_lat_tiles, n_ctx_tiles=n_ctx_tiles,
                          batch_major_out=batch_major_out),
        grid=(n_tiles, nf + 1),
        in_specs=[
            zspec(half, up, 0), zspec(rows, same, 0), zspec(half, dn, 0),
            zspec(half, up, nf), zspec(rows, same, nf), zspec(half, dn, nf),
            pl.BlockSpec((9, fb), lambda i, j: (0, cur(j))),
            pl.BlockSpec((9, fb), lambda i, j: (0, nf + cur(j))),
            pl.BlockSpec((None, fb, d), lambda i, j: (layer, prv(j), 0)),
            pl.BlockSpec((tt, b, d), lambda i, j: (i, 0, 0)),
            pl.BlockSpec((None, None, b, d), lambda i, j: (layer, sel(i), 0, 5)),
            _const_spec((1, d)),
        ],
        out_specs=(pl.BlockSpec((b, tt, d), lambda i, j: (0, i, 0)) if batch_major_out
                   else pl.BlockSpec((tt, b, d), lambda i, j: (i, 0, 0))),
        out_shape=jax.ShapeDtypeStruct((b, n_tiles * tt, d) if batch_major_out else xs.shape, F32),
        scratch_shapes=([pltpu.VMEM((rows, d), F32)] if batch_major_out else []) + [
            pltpu.VMEM((2, rows, fb), BF16),
            pltpu.VMEM((36, b, fb), F32),
        ] + [pltpu.VMEM((GRID_W + 2, b, fb), F32)] * 8,
        input_output_aliases={} if batch_major_out else {9: 0},
        compiler_params=_params("arbitrary", "arbitrary"),
        name="ffn_down",
    )(z, z, z, z, z, z, w_conv9, w_conv9, w_down, xs, mod, g_post)


def _pick(limit, total):
    t = min(limit, total)
    while total % t:
        t -= 1
    return t


def kernel(x, c, ctx, c_ctx, w_ada, b_ada, w_in, w_pool, pool_scale, ssm_a_re, ssm_a_im, ssm_log_dt,
           ssm_b_re, ssm_b_im, ssm_c_re, ssm_c_im, ssm_d, w_glu, w_out, g_pre_mix, g_post_mix,
           g_pre_ffn, g_post_ffn, w_up, w_conv, w_down):
    bsz, seq, d = x.shape
    n_ctx = ctx.shape[1]
    depth = w_ada.shape[0]
    mixw = w_in.shape[-1]
    sw = ssm_d.shape[-1]
    f = w_down.shape[1]
    assert bsz == SUBLANES and seq % GRID_W == 0 and n_ctx % GRID_W == 0
    assert sw % LANES == 0 and (mixw - sw) % sw == 0 and LANES % ssm_b_re.shape[-1] == 0
    s_all = seq + n_ctx
    tt = GRID_W
    n_lat_tiles = seq // tt
    n_ctx_tiles = n_ctx // tt
    n_tiles = n_lat_tiles + n_ctx_tiles
    tt_up = 2 * GRID_W
    assert seq % tt_up == 0 and n_ctx % tt_up == 0
    nb_up = _pick(1024, 2 * f)
    fb_in_place = _pick(512, f)
    fb_fresh = _pick(256, f)
    lat_chunks = _pick(128, seq // SSM_CHUNK)
    ctx_chunks = _pick(64, n_ctx // SSM_CHUNK)
    assert (seq // SSM_CHUNK) % ctx_chunks == 0

    c16 = jnp.concatenate([c, jnp.broadcast_to(c_ctx[None, :], (SUBLANES, d))], axis=0)
    mod_all = _ada_all(c16, w_ada, b_ada, _pick(1024, w_ada.shape[-1]))

    row = lambda v: v.reshape(1, -1).astype(F32)
    n_pool = len(POOL_WINDOWS)
    pw_all = mixw - sw
    pg = pw_all // n_pool
    w_pool_scaled = w_pool.astype(F32) * pool_scale.astype(F32).reshape(depth, n_pool, 1, pg)
    w_out_pool = jnp.einsum('lgio,lgod->lgid', w_pool_scaled,
                            w_out[:, :pw_all].astype(F32).reshape(depth, n_pool, pg, d),
                            precision=lax.Precision.HIGHEST).reshape(depth, pw_all, d)
    w_out = jnp.concatenate([w_out_pool, w_out[:, pw_all:].astype(F32)], axis=1)
    w_in, w_glu, w_out, w_up, w_down = (v.astype(BF16) for v in (w_in, w_glu, w_out, w_up, w_down))
    ssm_params = (ssm_a_re, ssm_a_im, ssm_log_dt, ssm_b_re, ssm_b_im, ssm_c_re, ssm_c_im)
    mats = jax.vmap(jax.vmap(_ssm_matrices))(*ssm_params)
    nbk = sw // LANES
    lane0 = (mixw - sw) // LANES
    h0 = jnp.zeros((2, nbk, bsz, mats[0].shape[-1]), F32)
    for l in range(depth):
        last = l == depth - 1
        if l == 0:
            xs, u = _premix_first(x, ctx, mod_all, row(g_pre_mix[l]), w_in, l, tt)
        else:
            u = _premix(xs, mod_all, row(g_pre_mix[l]), w_in, l, tt, n_lat_tiles)
        u5 = u.reshape(s_all // SSM_CHUNK, SSM_CHUNK, bsz, mixw)
        y_ctx, h_ctx = _ssm(u5, mats, l, h0, ctx_chunks, (seq // SSM_CHUNK) // ctx_chunks,
                            (n_ctx // SSM_CHUNK) // ctx_chunks, lane0)
        y_lat, _ = _ssm(u5, mats, l, h_ctx, lat_chunks, 0, (seq // SSM_CHUNK) // lat_chunks, lane0)

        nt = n_lat_tiles if last else n_tiles
        xs, h_ffn = _mix(u, y_lat.reshape(2, seq, bsz, sw), y_ctx.reshape(2, n_ctx, bsz, sw), xs,
                         mod_all, row(ssm_d[l]), w_glu, w_out, row(g_post_mix[l]), row(g_pre_ffn[l]), l,
                         tt, nt, n_lat_tiles, seq, n_ctx)
        nt_up = (seq if last else s_all) // tt_up
        fb = fb_fresh if last else fb_in_place
        z = _ffn_up(h_ffn, w_up, l, _pick(2048, math.gcd(seq, n_ctx) * bsz), nb_up, fb)
        xs = _ffn_down(z, w_conv[l].reshape(9, 2 * f).astype(F32), w_down, xs, mod_all,
                       row(g_post_ffn[l]), l, fb, nt_up, seq // tt_up, n_ctx // tt_up, last)
    return xs
```
